```python
import jax, jax.numpy as jnp
from jax import lax
import numpy as np

D_MODEL = 1024
BATCH = 8
SEQ = 2048
DEPTH = 4
DEC_BATCH = 32
DEC_SEQ = 4
PAST_LEN = 8192
PAGE_SIZE = 128

HEAD_DIM = 64
D_CONV = D_MODEL // 4
D_HGRN = (3 * D_MODEL) // 8
D_SB = D_MODEL - D_CONV - D_HGRN
N_HGRN = D_HGRN // HEAD_DIM
N_SB = D_SB // HEAD_DIM
D_MIX = D_CONV + D_HGRN + D_SB
D_IN = 2 * D_CONV + 4 * D_HGRN + 3 * D_SB
CONV_WIDTH = 31
D_FF = 2816
D_PLE = 256
N_NORMS = 7
HGRN_CHUNK = 64
SB_BLOCK = 128
SB_BIAS_INIT = -6.0
EPS = 1e-6

kernel_name = 'hymba_conv_hgrn2_stickbreaking_step'


def rms_norm(x, w):
    x32 = x.astype(jnp.float32)
    y = x32 * lax.rsqrt(jnp.mean(x32 * x32, axis=-1, keepdims=True) + EPS)
    return (y * w.astype(jnp.float32)).astype(x.dtype)


def head_rms_norm(x, w):
    x32 = x.astype(jnp.float32)
    y = x32 * lax.rsqrt(jnp.mean(x32 * x32, axis=-1, keepdims=True) + EPS)
    return y * w.reshape(x.shape[-2], x.shape[-1]).astype(jnp.float32)


def swiglu(x, wg, wu, wd):
    return (jax.nn.silu(x @ wg) * (x @ wu)) @ wd


def split_proj(proj):
    sizes = (D_CONV, D_CONV, D_HGRN, D_HGRN, D_HGRN, D_HGRN, D_SB, D_SB, D_SB)
    out = []
    off = 0
    for s in sizes:
        out.append(proj[..., off:off + s])
        off += s
    return out


def conformer_conv(a, b, prev, w, bias, ln_w, ln_b):
    u = a * jax.nn.sigmoid(b)
    u_pad = jnp.concatenate([prev.astype(u.dtype), u], axis=1)
    y = lax.conv_general_dilated(u_pad, w[:, None, :].astype(u.dtype), (1,), 'VALID',
                                 dimension_numbers=('NWC', 'WIO', 'NWC'),
                                 feature_group_count=u.shape[-1]) + bias
    y32 = y.astype(jnp.float32)
    mu = jnp.mean(y32, axis=-1, keepdims=True)
    var = jnp.mean(jnp.square(y32 - mu), axis=-1, keepdims=True)
    yn = (y32 - mu) * lax.rsqrt(var + EPS) * ln_w.astype(jnp.float32) + ln_b.astype(jnp.float32)
    return jax.nn.silu(yn).astype(a.dtype), u_pad[:, -(CONV_WIDTH - 1):]


def hgrn2_lower_bound(lb_logits, l):
    cs = jnp.cumsum(jax.nn.softmax(lb_logits.astype(jnp.float32), axis=0), axis=0)
    return cs[l] - cs[0]


def hgrn2_chunk_step(S, inp):
    q, log_f, k, v = inp
    c = q.shape[2]
    bcum = jnp.cumsum(log_f, axis=2)
    o_inter = jnp.einsum('bhtk,bhkv->bhtv', q * jnp.exp(bcum), S)
    diff = bcum[:, :, :, None, :] - bcum[:, :, None, :, :]
    causal = jnp.tril(jnp.ones((c, c), dtype=bool))[:, :, None]
    decay = jnp.exp(jnp.where(causal, diff, -jnp.inf))
    A = jnp.einsum('bhtk,bhsk,bhtsk->bhts', q, k, decay)
    o = o_inter + jnp.einsum('bhts,bhsv->bhtv', A, v)
    b_last = bcum[:, :, -1]
    S_new = jnp.exp(b_last)[..., None] * S + jnp.einsum(
        'bhsk,bhsv->bhkv', k * jnp.exp(b_last[:, :, None] - bcum), v)
    return S_new, o


def hgrn2_recurrence(q, log_f, k, v, S0):
    B, T, H, d = q.shape
    c = HGRN_CHUNK if T % HGRN_CHUNK == 0 else T
    n = T // c

    def blocks(x):
        return x.reshape(B, n, c, H, x.shape[-1]).transpose(1, 0, 3, 2, 4)

    S_T, o = lax.scan(hgrn2_chunk_step, S0, (blocks(q), blocks(log_f), blocks(k), blocks(v)))
    o = o.transpose(1, 0, 3, 2, 4).reshape(B, T, H, v.shape[-1])
    return o, S_T


def hgrn2_mixer(q, zf, i, g, lb, norm_w, S0):
    B, T, _ = q.shape
    shp = (B, T, N_HGRN, HEAD_DIM)
    log_f = jnp.logaddexp(jnp.log(lb), jnp.log1p(-lb) + jax.nn.log_sigmoid(zf.astype(jnp.float32)))
    k = -jnp.expm1(log_f)
    o, S = hgrn2_recurrence(q.astype(jnp.float32).reshape(shp), log_f.reshape(shp),
                            k.reshape(shp), i.astype(jnp.float32).reshape(shp),
                            S0.astype(jnp.float32))
    o = head_rms_norm(o, norm_w).reshape(B, T, D_HGRN) * jax.nn.silu(g.astype(jnp.float32))
    return o.astype(q.dtype), S


def sb_block(qb, qpos, k, v, kpos, bias):
    z = jnp.einsum('bqhd,bkhd->bhqk', qb, k, preferred_element_type=jnp.float32) * (HEAD_DIM ** -0.5)
    z = z + bias.astype(jnp.float32)[None, :, None, None]
    mask = (kpos[None, :] < qpos[:, None])[None, None]
    log_1mb = jnp.where(mask, jax.nn.log_sigmoid(-z), 0.0)
    between = lax.cumsum(log_1mb, axis=3, reverse=True) - log_1mb
    A = jnp.where(mask, jnp.exp(jax.nn.log_sigmoid(z) + between), 0.0)
    return jnp.einsum('bhqk,bkhd->bqhd', A, v.astype(jnp.float32))


def stick_breaking_attention(q, k_all, v_all, past_len, bias):
    B, T, H, d = q.shape
    c = SB_BLOCK if T % SB_BLOCK == 0 else T
    n = T // c
    kpos = jnp.arange(k_all.shape[1])
    qpos = (past_len + jnp.arange(T)).reshape(n, c)
    qb = q.reshape(B, n, c, H, d).transpose(1, 0, 2, 3, 4)
    o = lax.map(lambda a: sb_block(a[0], a[1], k_all, v_all, kpos, bias), (qb, qpos))
    return o.transpose(1, 0, 2, 3, 4).reshape(B, T, H, d)


def mixer(xn, l, conv_prev, hgrn_prev, k_past, v_past, prm):
    B, T, _ = xn.shape
    a, b, hq, hf, hi, hg, sq, sk, sv = split_proj(xn @ prm['w_in'][l])
    conv_out, conv_state = conformer_conv(a, b, conv_prev, prm['conv_w'][l], prm['conv_b'][l],
                                          prm['conv_ln_w'][l], prm['conv_ln_b'][l])
    lb = hgrn2_lower_bound(prm['hgrn_lb_logits'], l)
    hgrn_out, hgrn_state = hgrn2_mixer(hq, hf, hi, hg, lb, prm['hgrn_norm_w'][l], hgrn_prev)
    hs = (B, T, N_SB, HEAD_DIM)
    sq, sk, sv = sq.reshape(hs), sk.reshape(hs), sv.reshape(hs)
    if k_past is None:
        k_all, v_all, past_len = sk, sv, 0
    else:
        k_all = jnp.concatenate([k_past.astype(sk.dtype), sk], axis=1)
        v_all = jnp.concatenate([v_past.astype(sv.dtype), sv], axis=1)
        past_len = k_past.shape[1]
    sb = stick_breaking_attention(sq, k_all, v_all, past_len, prm['sb_bias'][l])
    sb_out = head_rms_norm(sb, prm['sb_norm_w'][l]).reshape(B, T, D_SB).astype(xn.dtype)
    cat = jnp.concatenate([conv_out, hgrn_out, sb_out], axis=-1)
    return cat @ prm['w_out'][l], (sk, sv, conv_state, hgrn_state)


def trunk_layer(h, p, l, conv_prev, hgrn_prev, k_past, v_past, prm):
    nw = prm['norm_w'][l]
    h = h + 0.5 * rms_norm(swiglu(rms_norm(h, nw[0]), prm['ffn_w_gate'][l, 0],
                                  prm['ffn_w_up'][l, 0], prm['ffn_w_down'][l, 0]), nw[1])
    mix, state = mixer(rms_norm(h, nw[2]), l, conv_prev, hgrn_prev, k_past, v_past, prm)
    h = h + rms_norm(mix, nw[3])
    h = h + 0.5 * rms_norm(swiglu(rms_norm(h, nw[4]), prm['ffn_w_gate'][l, 1],
                                  prm['ffn_w_up'][l, 1], prm['ffn_w_down'][l, 1]), nw[5])
    ple = (p.astype(h.dtype) @ prm['ple_w_up'][l]) * jax.nn.sigmoid(h @ prm['ple_w_gate'][l])
    h = h + rms_norm(ple, nw[6])
    return h, state


def setup_inputs(seed: int = 0) -> dict:
    key = jax.random.key(seed)
    ks = jax.random.split(key, 32)
    n_pages = PAST_LEN // PAGE_SIZE
    n_pool = (5 * DEC_BATCH * n_pages) // 4
    nrm = jax.random.normal
    f32 = jnp.float32
    page_table = jax.random.permutation(ks[0], n_pool)[:DEC_BATCH * n_pages].reshape(
        DEC_BATCH, n_pages).astype(jnp.int32)
    return {
        'x_prompt': nrm(ks[1], (BATCH, SEQ, D_MODEL), f32),
        'x_sample': nrm(ks[2], (DEC_BATCH, DEC_SEQ, D_MODEL), f32),
        'cache_k': nrm(ks[3], (DEPTH, n_pool, PAGE_SIZE, N_SB, HEAD_DIM), f32),
        'cache_v': nrm(ks[4], (DEPTH, n_pool, PAGE_SIZE, N_SB, HEAD_DIM), f32),
        'state_conv': 0.5 * nrm(ks[5], (DEPTH, DEC_BATCH, CONV_WIDTH - 1, D_CONV), f32),
        'state_hgrn': 0.5 * nrm(ks[6], (DEPTH, DEC_BATCH, N_HGRN, HEAD_DIM, HEAD_DIM), f32),
        'page_table': page_table,
        'p_prompt': nrm(ks[7], (DEPTH, BATCH, SEQ, D_PLE), f32),
        'p_sample': nrm(ks[8], (DEPTH, DEC_BATCH, DEC_SEQ, D_PLE), f32),
        'norm_w': 1.0 + 0.05 * nrm(ks[9], (DEPTH, N_NORMS, D_MODEL), f32),
        'ffn_w_gate': nrm(ks[10], (DEPTH, 2, D_MODEL, D_FF), f32) * D_MODEL ** -0.5,
        'ffn_w_up': nrm(ks[11], (DEPTH, 2, D_MODEL, D_FF), f32) * D_MODEL ** -0.5,
        'ffn_w_down': nrm(ks[12], (DEPTH, 2, D_FF, D_MODEL), f32) * D_FF ** -0.5,
        'w_in': nrm(ks[13], (DEPTH, D_MODEL, D_IN), f32) * D_MODEL ** -0.5,
        'conv_w': nrm(ks[14], (DEPTH, CONV_WIDTH, D_CONV), f32) * CONV_WIDTH ** -0.5,
        'conv_b': 0.01 * nrm(ks[15], (DEPTH, D_CONV), f32),
        'conv_ln_w': 1.0 + 0.05 * nrm(ks[16], (DEPTH, D_CONV), f32),
        'conv_ln_b': 0.01 * nrm(ks[17], (DEPTH, D_CONV), f32),
        'hgrn_lb_logits': 0.5 * nrm(ks[18], (DEPTH, D_HGRN), f32),
        'hgrn_norm_w': 1.0 + 0.05 * nrm(ks[19], (DEPTH, D_HGRN), f32),
        'sb_bias': SB_BIAS_INIT + 0.1 * nrm(ks[24], (DEPTH, N_SB), f32),
        'sb_norm_w': 1.0 + 0.05 * nrm(ks[20], (DEPTH, D_SB), f32),
        'w_out': nrm(ks[21], (DEPTH, D_MIX, D_MODEL), f32) * D_MIX ** -0.5,
        'ple_w_up': nrm(ks[22], (DEPTH, D_PLE, D_MODEL), f32) * D_PLE ** -0.5,
        'ple_w_gate': nrm(ks[23], (DEPTH, D_MODEL, D_MODEL), f32) * D_MODEL ** -0.5,
    }


def reference(x_prompt, x_sample, cache_k, cache_v, state_conv, state_hgrn, page_table,
              p_prompt, p_sample, norm_w, ffn_w_gate, ffn_w_up, ffn_w_down, w_in, conv_w,
              conv_b, conv_ln_w, conv_ln_b, hgrn_lb_logits, hgrn_norm_w, sb_bias, sb_norm_w,
              w_out, ple_w_up, ple_w_gate):
    prm = dict(norm_w=norm_w, ffn_w_gate=ffn_w_gate, ffn_w_up=ffn_w_up, ffn_w_down=ffn_w_down,
               w_in=w_in, conv_w=conv_w, conv_b=conv_b, conv_ln_w=conv_ln_w, conv_ln_b=conv_ln_b,
               hgrn_lb_logits=hgrn_lb_logits, hgrn_norm_w=hgrn_norm_w, sb_bias=sb_bias,
               sb_norm_w=sb_norm_w, w_out=w_out, ple_w_up=ple_w_up, ple_w_gate=ple_w_gate)
    b_p = x_prompt.shape[0]
    b_s = x_sample.shape[0]
    n_pages = page_table.shape[1]
    conv0 = jnp.zeros((b_p, CONV_WIDTH - 1, D_CONV), x_prompt.dtype)
    hgrn0 = jnp.zeros((b_p, N_HGRN, HEAD_DIM, HEAD_DIM), jnp.float32)
    h_p, h_s = x_prompt, x_sample
    kp, vp, cp, sp, ksm, vsm, csm, ssm = [], [], [], [], [], [], [], []
    for l in range(DEPTH):
        h_p, (k1, v1, c1, s1) = trunk_layer(h_p, p_prompt[l], l, conv0, hgrn0, None, None, prm)
        k_past = cache_k[l][page_table].reshape(b_s, n_pages * PAGE_SIZE, N_SB, HEAD_DIM)
        v_past = cache_v[l][page_table].reshape(b_s, n_pages * PAGE_SIZE, N_SB, HEAD_DIM)
        h_s, (k2, v2, c2, s2) = trunk_layer(h_s, p_sample[l], l, state_conv[l], state_hgrn[l],
                                            k_past, v_past, prm)
        kp.append(k1); vp.append(v1); cp.append(c1); sp.append(s1)
        ksm.append(k2); vsm.append(v2); csm.append(c2); ssm.append(s2)
    return (h_p, h_s, jnp.stack(kp), jnp.stack(vp), jnp.stack(cp), jnp.stack(sp),
            jnp.stack(ksm), jnp.stack(vsm), jnp.stack(csm), jnp.stack(ssm))
```

```python
import functools

import numpy as np
import jax
import jax.numpy as jnp
from jax import lax
from jax.experimental import pallas as pl
from jax.experimental.pallas import tpu as pltpu

F32 = jnp.float32
BF16 = jnp.bfloat16

EPS = 1e-6
HEAD_DIM = 64
LANES = 128
CONV_WIDTH = 31
CONV_PAD = 32
HGRN_CHUNK = 64
SB_BLOCK = 128
PAGES_PER_STEP = 8
VMEM_LIMIT = 48 * 1024 * 1024


def _cparams(*sem):
    return pltpu.CompilerParams(dimension_semantics=sem, vmem_limit_bytes=VMEM_LIMIT)


def _rms(x, w):
    ms = jnp.mean(x * x, axis=-1, keepdims=True)
    return x * lax.rsqrt(ms + EPS) * w


def _sigmoid(x):
    return 1.0 / (1.0 + jnp.exp(-x))


def _softplus(x):
    return jnp.maximum(x, 0.0) + jnp.log1p(jnp.exp(-jnp.abs(x)))


def _dot(a, b):
    return jnp.dot(a, b, preferred_element_type=F32)


def _dot_nt(a, b):
    return lax.dot_general(a, b, (((1,), (1,)), ((), ())), preferred_element_type=F32)


def _dot_tn(a, b):
    return lax.dot_general(a, b, (((0,), (0,)), ((), ())), preferred_element_type=F32)


def _split(x):
    hi = x.astype(BF16)
    lo = (x - hi.astype(F32)).astype(BF16)
    return hi, lo


def _dot_split(x, w):
    hi, lo = _split(x)
    return _dot(hi, w) + _dot(lo, w)


def _head_rms(o, g_ref, w):
    ms = _dot_split(o * o, g_ref[...])
    return o * lax.rsqrt(ms + EPS) * w


def _row_tile(m):
    for t in (1024, 512, 256, 128, 64, 32, 16, 8):
        if m % t == 0:
            return t
    raise ValueError(f"token count {m} must be a multiple of 8")


def _ffn_kernel(h_ref, nw_ref, wg_ref, wu_ref, wd_ref, o_ref, xn_ref, acc_ref, *, pre, post, nf):
    f = pl.program_id(1)

    @pl.when(f == 0)
    def _():
        xn_ref[...] = _rms(h_ref[...], nw_ref[pre:pre + 1, :]).astype(BF16)
        acc_ref[...] = jnp.zeros_like(acc_ref)

    xn = xn_ref[...]
    g = _dot(xn, wg_ref[...])
    u = _dot(xn, wu_ref[...])
    a = (g * _sigmoid(g) * u).astype(BF16)
    acc_ref[...] += _dot(a, wd_ref[...])

    @pl.when(f == nf - 1)
    def _():
        o_ref[...] = h_ref[...] + 0.5 * _rms(acc_ref[...], nw_ref[post:post + 1, :])


def _ffn(h, nw, wg, wu, wd, l, i, pre, post):
    m, d = h.shape
    dff = wg.shape[-1]
    tm = _row_tile(m)
    tf = 256
    nf = dff // tf
    return pl.pallas_call(
        functools.partial(_ffn_kernel, pre=pre, post=post, nf=nf),
        out_shape=jax.ShapeDtypeStruct((m, d), F32),
        grid=(m // tm, nf),
        in_specs=[
            pl.BlockSpec((tm, d), lambda r, f: (r, 0)),
            pl.BlockSpec((None,) + nw.shape[1:], lambda r, f: (l, 0, 0)),
            pl.BlockSpec((None, None, d, tf), lambda r, f: (l, i, 0, f)),
            pl.BlockSpec((None, None, d, tf), lambda r, f: (l, i, 0, f)),
            pl.BlockSpec((None, None, tf, d), lambda r, f: (l, i, f, 0)),
        ],
        out_specs=pl.BlockSpec((tm, d), lambda r, f: (r, 0)),
        scratch_shapes=[pltpu.VMEM((tm, d), BF16), pltpu.VMEM((tm, d), F32)],
        compiler_params=_cparams("parallel", "arbitrary"),
        name="ffn",
    )(h, nw, wg, wu, wd)


def _proj_kernel(h_ref, nw_ref, w_ref, conv_ref, hg_ref, q_ref, k_ref, v_ref, *, bounds):
    xn = _rms(h_ref[...], nw_ref[2:3, :]).astype(BF16)
    for ref, (lo, hi) in zip((conv_ref, hg_ref, q_ref, k_ref, v_ref), bounds):
        ref[...] = _dot(xn, w_ref[:, lo:hi])


def _proj(h, nw, w_in, l, widths):
    m, d = h.shape
    tm = min(_row_tile(m), 512)
    bounds, off = [], 0
    for w in widths:
        bounds.append((off, off + w))
        off += w
    return pl.pallas_call(
        functools.partial(_proj_kernel, bounds=tuple(bounds)),
        out_shape=[jax.ShapeDtypeStruct((m, w), F32) for w in widths],
        grid=(m // tm,),
        in_specs=[
            pl.BlockSpec((tm, d), lambda r: (r, 0)),
            pl.BlockSpec((None,) + nw.shape[1:], lambda r: (l, 0, 0)),
            pl.BlockSpec((None,) + w_in.shape[1:], lambda r: (l, 0, 0)),
        ],
        out_specs=[pl.BlockSpec((tm, w), lambda r: (r, 0)) for w in widths],
        compiler_params=_cparams("parallel"),
        name="proj",
    )(h, nw, w_in)


def _conv_kernel(ab_ref, prev_ref, w_ref, b_ref, lnw_ref, lnb_ref, y_ref, st_ref, up_ref, *, t, rt, dc):
    npad = CONV_WIDTH - 1
    first = CONV_PAD - npad
    a = ab_ref[:, 0:dc]
    b = ab_ref[:, dc:2 * dc]
    up_ref[0:first, :] = jnp.zeros((first, dc), F32)
    up_ref[first:CONV_PAD, :] = prev_ref[...]
    up_ref[CONV_PAD:CONV_PAD + t, :] = a * _sigmoid(b)
    st_ref[...] = up_ref[first + t:first + t + npad, :]
    for r in range(t // rt):
        base = r * rt + first
        acc = jnp.zeros((rt, dc), F32)
        for j in range(CONV_WIDTH):
            acc = acc + w_ref[j:j + 1, :] * up_ref[base + j:base + j + rt, :]
        y = acc + b_ref[...]
        mu = jnp.mean(y, axis=-1, keepdims=True)
        yc = y - mu
        var = jnp.mean(yc * yc, axis=-1, keepdims=True)
        yn = yc * lax.rsqrt(var + EPS) * lnw_ref[...] + lnb_ref[...]
        y_ref[r * rt:(r + 1) * rt, :] = yn * _sigmoid(yn)


def _conv(ab, prev, w, b, lnw, lnb, l):
    bsz, t, dc2 = ab.shape
    dc = dc2 // 2
    rt = min(t, 256)
    vec = lambda a: pl.BlockSpec((None, 1, dc), lambda s: (l, 0, 0))
    return pl.pallas_call(
        functools.partial(_conv_kernel, t=t, rt=rt, dc=dc),
        out_shape=[jax.ShapeDtypeStruct((bsz, t, dc), F32),
                   jax.ShapeDtypeStruct((bsz, CONV_WIDTH - 1, dc), F32)],
        grid=(bsz,),
        in_specs=[
            pl.BlockSpec((None, t, dc2), lambda s: (s, 0, 0)),
            pl.BlockSpec((None, CONV_WIDTH - 1, dc), lambda s: (s, 0, 0)),
            pl.BlockSpec((None, CONV_WIDTH, dc), lambda s: (l, 0, 0)),
            vec(b), vec(lnw), vec(lnb),
        ],
        out_specs=[pl.BlockSpec((None, t, dc), lambda s: (s, 0, 0)),
                   pl.BlockSpec((None, CONV_WIDTH - 1, dc), lambda s: (s, 0, 0))],
        scratch_shapes=[pltpu.VMEM((CONV_PAD + t, dc), F32)],
        compiler_params=_cparams("parallel"),
        name="conv",
    )(ab, prev, w, b.reshape(b.shape[0], 1, dc), lnw.reshape(lnw.shape[0], 1, dc),
      lnb.reshape(lnb.shape[0], 1, dc))


def _hgrn_consts(c):
    levels = []
    m = c // 2
    while m >= 1:
        levels.append(m)
        m //= 2
    idx = np.arange(c)
    mats = [(idx[None, :] <= idx[:, None]).astype(np.float32)]
    masks = []
    for m in levels:
        blk, half = idx // (2 * m), (idx // m) % 2
        same = blk[:, None] == blk[None, :]
        upper = same & (half[:, None] == 1) & (half[None, :] == 1) & (idx[None, :] <= idx[:, None])
        lower = same & (half[:, None] == 0) & (half[None, :] == 0) & (idx[None, :] > idx[:, None])
        mats += [upper.astype(np.float32), lower.astype(np.float32)]
        masks.append((same & (half[:, None] == 1) & (half[None, :] == 0)).astype(np.float32))
    masks.append(np.eye(c, dtype=np.float32))
    masks = np.stack(masks)
    masks = np.concatenate([masks, masks], axis=1)
    return len(levels), np.concatenate(mats, axis=0), masks


def _head_avg(d):
    i = np.arange(d) // HEAD_DIM
    return (i[:, None] == i[None, :]).astype(np.float32) / HEAD_DIM


def _hgrn_kernel(hg_ref, s0_ref, lbl_ref, nw_ref, mall_ref, mask_ref, g_ref, o_ref, sT_ref, st_ref,
                 *, l, c, d, nlev, nc, t_valid):
    ci = pl.program_id(1)
    npair = d // LANES

    @pl.when(ci == 0)
    def _():
        st_ref[...] = s0_ref[...]

    lg = lbl_ref[...]
    e = jnp.exp(lg - jnp.max(lg, axis=0, keepdims=True))
    sm = e / jnp.sum(e, axis=0, keepdims=True)
    row = lax.broadcasted_iota(jnp.int32, sm.shape, 0)
    lb = jnp.sum(jnp.where((row >= 1) & (row <= l), sm, 0.0), axis=0, keepdims=True)

    q = hg_ref[:, 0:d]
    zf = hg_ref[:, d:2 * d]
    v = hg_ref[:, 2 * d:3 * d]
    gate = hg_ref[:, 3 * d:4 * d]

    sp = _softplus(-zf)
    a = jnp.log(lb)
    bb = jnp.log1p(-lb) - sp
    log_f = jnp.maximum(a, bb) + jnp.log1p(jnp.exp(-jnp.abs(a - bb)))
    kk = (1.0 - lb) * _sigmoid(-zf)
    if t_valid < nc * c:
        trow = lax.broadcasted_iota(jnp.int32, (c, d), 0) + ci * c
        live = trow < t_valid
        log_f = jnp.where(live, log_f, 0.0)
        kk = jnp.where(live, kk, 0.0)

    lf_hi, lf_lo = _split(log_f)
    mall = mall_ref[...]
    e_all = _dot(mall, lf_hi) + _dot(mall, lf_lo)
    bcum = e_all[0:c]
    b_last = bcum[c - 1:c, :]
    qe = (q * jnp.exp(bcum)).astype(BF16)
    kd = (kk * jnp.exp(b_last - bcum)).astype(BF16)
    v16 = v.astype(BF16)
    ones16 = jnp.ones((c, LANES), BF16)
    lane = lax.broadcasted_iota(jnp.int32, (c, LANES), 1)
    first = lane < HEAD_DIM
    ri = lax.broadcasted_iota(jnp.int32, (LANES, LANES), 0) // HEAD_DIM
    cj = lax.broadcasted_iota(jnp.int32, (LANES, LANES), 1) // HEAD_DIM
    blockdiag = ri == cj

    outs = []
    for p in range(npair):
        sl = slice(p * LANES, (p + 1) * LANES)
        qp, kp = q[:, sl], kk[:, sl]
        amat = jnp.zeros((2 * c, c), F32)
        for i in range(nlev + 1):
            if i < nlev:
                ql = qp * jnp.exp(e_all[(1 + 2 * i) * c:(2 + 2 * i) * c, sl])
                kl = kp * jnp.exp(e_all[(2 + 2 * i) * c:(3 + 2 * i) * c, sl])
            else:
                ql, kl = qp, kp
            qst = jnp.concatenate([jnp.where(first, ql, 0.0), jnp.where(first, 0.0, ql)], axis=0)
            amat = amat + mask_ref[i] * _dot_nt(qst.astype(BF16), kl.astype(BF16))
        a16 = amat.astype(BF16)
        vp = v16[:, sl]
        zero = jnp.zeros_like(vp)
        state = st_ref[p]
        o = _dot(qe[:, sl], state.astype(BF16))
        o = o + _dot(a16[0:c], jnp.where(first, vp, zero)) + _dot(a16[c:2 * c], jnp.where(first, zero, vp))
        outs.append(o)
        elog = _dot_tn(lf_hi[:, sl], ones16) + _dot_tn(lf_lo[:, sl], ones16)
        upd = _dot_tn(kd[:, sl], vp)
        st_ref[p] = jnp.exp(elog) * state + jnp.where(blockdiag, upd, 0.0)

    o = jnp.concatenate(outs, axis=1)
    o_ref[...] = _head_rms(o, g_ref, nw_ref[...]) * (gate * _sigmoid(gate))

    @pl.when(ci == nc - 1)
    def _():
        sT_ref[...] = st_ref[...]


def _hgrn(hg, s0p, lb_logits, norm_w, l, t_valid):
    bsz, t, d4 = hg.shape
    d = d4 // 4
    c = HGRN_CHUNK
    nc = t // c
    nlev, mall, masks = _hgrn_consts(c)
    npair = d // LANES
    full = lambda a: pl.BlockSpec(a.shape, lambda s, ci: (0,) * a.ndim)
    mall = jnp.asarray(mall, BF16)
    masks = jnp.asarray(masks, F32)
    gavg = jnp.asarray(_head_avg(d), BF16)
    return pl.pallas_call(
        functools.partial(_hgrn_kernel, l=l, c=c, d=d, nlev=nlev, nc=nc, t_valid=t_valid),
        out_shape=[jax.ShapeDtypeStruct((bsz, t, d), F32),
                   jax.ShapeDtypeStruct((bsz, npair, LANES, LANES), F32)],
        grid=(bsz, nc),
        in_specs=[
            pl.BlockSpec((None, c, d4), lambda s, ci: (s, ci, 0)),
            pl.BlockSpec((None, npair, LANES, LANES), lambda s, ci: (s, 0, 0, 0)),
            full(lb_logits),
            pl.BlockSpec((None, 1, d), lambda s, ci: (l, 0, 0)),
            full(mall), full(masks), full(gavg),
        ],
        out_specs=[pl.BlockSpec((None, c, d), lambda s, ci: (s, ci, 0)),
                   pl.BlockSpec((None, npair, LANES, LANES), lambda s, ci: (s, 0, 0, 0))],
        scratch_shapes=[pltpu.VMEM((npair, LANES, LANES), F32)],
        compiler_params=_cparams("parallel", "arbitrary"),
        name="hgrn",
    )(hg, s0p, lb_logits, norm_w.reshape(norm_w.shape[0], 1, d), mall, masks, gavg)


def _pack_state(s):
    bsz, h = s.shape[:2]
    s = s.reshape(bsz, h // 2, 2, HEAD_DIM, HEAD_DIM)
    z = jnp.zeros_like(s[:, :, 0])
    top = jnp.concatenate([s[:, :, 0], z], axis=-1)
    bot = jnp.concatenate([z, s[:, :, 1]], axis=-1)
    return jnp.concatenate([top, bot], axis=-2)


def _unpack_state(sp):
    a = sp[:, :, :HEAD_DIM, :HEAD_DIM]
    b = sp[:, :, HEAD_DIM:, HEAD_DIM:]
    bsz, npair = sp.shape[:2]
    return jnp.stack([a, b], axis=2).reshape(bsz, 2 * npair, HEAD_DIM, HEAD_DIM)


def _sb_consts():
    j = np.arange(SB_BLOCK)
    after = (j[:, None] > j[None, :]).astype(np.float32)
    return np.concatenate([after, np.ones_like(after)], axis=1)


def _sb_tile(z, kv, mask, carry, uo_ref):
    sp = _softplus(z)
    log1mb = -sp
    logb = z - sp
    if mask is not None:
        log1mb = jnp.where(mask, log1mb, 0.0)
    hi, lo = _split(log1mb)
    r = z.shape[0]
    both = _dot(jnp.concatenate([hi, lo], axis=0), uo_ref[...])
    both = both[0:r] + both[r:2 * r]
    between = carry + both[:, 0:SB_BLOCK]
    a = jnp.exp(logb + between)
    if mask is not None:
        a = jnp.where(mask, a, 0.0)
    return _dot(a.astype(BF16), kv), carry + both[:, SB_BLOCK:2 * SB_BLOCK]


def _sbp_kernel(bias_ref, q_ref, k_ref, v_ref, nw_ref, uo_ref, g_ref, o_ref,
                k16_ref, ve_ref, vo_ref, acc_ref, car_ref, *, l, d):
    qi = pl.program_id(1)
    blk = SB_BLOCK
    npair = d // LANES
    lane_d = lax.broadcasted_iota(jnp.int32, (1, d), 1)
    even = (lane_d // HEAD_DIM) % 2 == 0

    @pl.when(qi == 0)
    def _():
        k16_ref[...] = k_ref[...].astype(BF16)
        vv = v_ref[...]
        ve_ref[...] = jnp.where(even, vv, 0.0).astype(BF16)
        vo_ref[...] = jnp.where(even, 0.0, vv).astype(BF16)

    first = lax.broadcasted_iota(jnp.int32, (blk, LANES), 1) < HEAD_DIM
    ti = lax.broadcasted_iota(jnp.int32, (blk, blk), 0)
    si = lax.broadcasted_iota(jnp.int32, (blk, blk), 1)
    causal = si < ti
    scale = HEAD_DIM ** -0.5
    q0 = pl.multiple_of(qi * blk, blk)

    for p in range(npair):
        sl = slice(p * LANES, (p + 1) * LANES)
        qp = q_ref[:, sl] * scale
        qh = (jnp.where(first, qp, 0.0).astype(BF16), jnp.where(first, 0.0, qp).astype(BF16))
        vrefs = (ve_ref, vo_ref)
        acc = jnp.zeros((blk, LANES), F32)
        for hh in range(2):
            z = _dot_nt(qh[hh], k16_ref[pl.ds(q0, blk), sl]) + bias_ref[l, 2 * p + hh]
            o, c = _sb_tile(z, vrefs[hh][pl.ds(q0, blk), sl], causal, jnp.zeros((blk, LANES), F32), uo_ref)
            acc = acc + o
            car_ref[hh] = c
        acc_ref[...] = acc

        def body(j, _):
            k0 = pl.multiple_of((qi - 1 - j) * blk, blk)
            kb = k16_ref[pl.ds(k0, blk), sl]
            tot = acc_ref[...]
            for hh in range(2):
                z = _dot_nt(qh[hh], kb) + bias_ref[l, 2 * p + hh]
                o, c = _sb_tile(z, vrefs[hh][pl.ds(k0, blk), sl], None, car_ref[hh], uo_ref)
                tot = tot + o
                car_ref[hh] = c
            acc_ref[...] = tot
            return 0

        lax.fori_loop(0, qi, body, 0)
        o_ref[:, sl] = acc_ref[...]

    o_ref[...] = _head_rms(o_ref[...], g_ref, nw_ref[...])


def _sb_prompt(q, k, v, bias, norm_w, l):
    bsz, t, d = q.shape
    blk = SB_BLOCK
    uo = jnp.asarray(_sb_consts(), BF16)
    gavg = jnp.asarray(_head_avg(d), BF16)
    full = lambda a: pl.BlockSpec(a.shape, lambda s, qi: (0,) * a.ndim)
    return pl.pallas_call(
        functools.partial(_sbp_kernel, l=l, d=d),
        out_shape=jax.ShapeDtypeStruct((bsz, t, d), F32),
        grid=(bsz, t // blk),
        in_specs=[
            pl.BlockSpec(memory_space=pltpu.SMEM),
            pl.BlockSpec((None, blk, d), lambda s, qi: (s, qi, 0)),
            pl.BlockSpec((None, t, d), lambda s, qi: (s, 0, 0)),
            pl.BlockSpec((None, t, d), lambda s, qi: (s, 0, 0)),
            pl.BlockSpec((None, 1, d), lambda s, qi: (l, 0, 0)),
            full(uo), full(gavg),
        ],
        out_specs=pl.BlockSpec((None, blk, d), lambda s, qi: (s, qi, 0)),
        scratch_shapes=[pltpu.VMEM((t, d), BF16), pltpu.VMEM((t, d), BF16), pltpu.VMEM((t, d), BF16),
                        pltpu.VMEM((blk, LANES), F32), pltpu.VMEM((2, blk, LANES), F32)],
        compiler_params=_cparams("parallel", "arbitrary"),
        name="sb_prompt",
    )(bias, q, k, v, norm_w.reshape(norm_w.shape[0], 1, d), uo, gavg)


def _sbs_kernel(pt_ref, bias_ref, q_ref, kn_ref, vn_ref, nw_ref, uo_ref, g_ref, *rest, l, d, nh, tq, nstep):
    npg = PAGES_PER_STEP
    kp_refs = rest[0:npg]
    vp_refs = rest[npg:2 * npg]
    o_ref, qbd_ref, acc_ref, car_ref = rest[2 * npg:]
    j = pl.program_id(1)
    rows = nh * tq
    blk = SB_BLOCK
    lane_d = lax.broadcasted_iota(jnp.int32, (rows, d), 1) // HEAD_DIM
    rgrp = lax.broadcasted_iota(jnp.int32, (rows, d), 0) // tq
    own = lane_d == rgrp

    rg1 = lax.broadcasted_iota(jnp.int32, (rows, blk), 0) // tq
    bias = jnp.zeros((rows, blk), F32)
    for h in range(nh):
        bias = jnp.where(rg1 == h, bias_ref[l, h], bias)

    @pl.when(j == 0)
    def _():
        qs = q_ref[...] * (HEAD_DIM ** -0.5)
        qbd = jnp.where(own, jnp.concatenate([qs] * nh, axis=0), 0.0).astype(BF16)
        qbd_ref[...] = qbd
        tpos = lax.broadcasted_iota(jnp.int32, (rows, blk), 0) % tq
        spos = lax.broadcasted_iota(jnp.int32, (rows, blk), 1)
        z = _dot_nt(qbd, kn_ref[...].astype(BF16)) + bias
        o, c = _sb_tile(z, vn_ref[...].astype(BF16), spos < tpos, jnp.zeros((rows, blk), F32), uo_ref)
        acc_ref[...] = o
        car_ref[...] = c

    qbd = qbd_ref[...]
    acc = acc_ref[...]
    c = car_ref[...]
    for i in range(npg):
        z = _dot_nt(qbd, kp_refs[i][...].astype(BF16)) + bias
        o, c = _sb_tile(z, vp_refs[i][...].astype(BF16), None, c, uo_ref)
        acc = acc + o
    acc_ref[...] = acc
    car_ref[...] = c

    @pl.when(j == nstep - 1)
    def _():
        full = jnp.where(own, acc_ref[...], 0.0)
        o = full[0:tq]
        for h in range(1, nh):
            o = o + full[h * tq:(h + 1) * tq]
        o_ref[...] = _head_rms(o, g_ref, nw_ref[...])


def _sb_sample(q, k_new, v_new, cache_k, cache_v, page_table, bias, norm_w, l):
    bsz, tq, d = q.shape
    nh = d // HEAD_DIM
    npages = page_table.shape[1]
    npg = PAGES_PER_STEP
    nstep = npages // npg
    blk = SB_BLOCK
    uo = jnp.asarray(_sb_consts(), BF16)
    gavg = jnp.asarray(_head_avg(d), BF16)
    full = lambda a: pl.BlockSpec(a.shape, lambda s, j, pt: (0,) * a.ndim)

    def page_spec(i):
        return pl.BlockSpec((None, None, blk, d),
                            lambda s, j, pt: (l, pt[s, npages - 1 - (j * npg + i)], 0, 0))

    grid_spec = pltpu.PrefetchScalarGridSpec(
        num_scalar_prefetch=1,
        grid=(bsz, nstep),
        in_specs=[
            pl.BlockSpec(memory_space=pltpu.SMEM),
            pl.BlockSpec((None, tq, d), lambda s, j, pt: (s, 0, 0)),
            pl.BlockSpec((None, blk, d), lambda s, j, pt: (s, 0, 0)),
            pl.BlockSpec((None, blk, d), lambda s, j, pt: (s, 0, 0)),
            pl.BlockSpec((None, 1, d), lambda s, j, pt: (l, 0, 0)),
            full(uo), full(gavg),
        ] + [page_spec(i) for i in range(npg)] + [page_spec(i) for i in range(npg)],
        out_specs=pl.BlockSpec((None, tq, d), lambda s, j, pt: (s, 0, 0)),
        scratch_shapes=[pltpu.VMEM((nh * tq, d), BF16), pltpu.VMEM((nh * tq, d), F32),
                        pltpu.VMEM((nh * tq, blk), F32)],
    )
    ck = cache_k.reshape(cache_k.shape[0], cache_k.shape[1], blk, d)
    cv = cache_v.reshape(cache_v.shape[0], cache_v.shape[1], blk, d)
    return pl.pallas_call(
        functools.partial(_sbs_kernel, l=l, d=d, nh=nh, tq=tq, nstep=nstep),
        out_shape=jax.ShapeDtypeStruct((bsz, tq, d), F32),
        grid_spec=grid_spec,
        compiler_params=_cparams("parallel", "arbitrary"),
        name="sb_sample",
    )(page_table, bias, q, k_new, v_new, norm_w.reshape(norm_w.shape[0], 1, d), uo, gavg,
      *([ck] * npg), *([cv] * npg))


def _mixout_kernel(h_ref, c_ref, g_ref, s_ref, nw_ref, w_ref, o_ref, *, dc, dh):
    mix = _dot(c_ref[...].astype(BF16), w_ref[0:dc, :])
    mix = mix + _dot(g_ref[...].astype(BF16), w_ref[dc:dc + dh, :])
    mix = mix + _dot(s_ref[...].astype(BF16), w_ref[dc + dh:, :])
    o_ref[...] = h_ref[...] + _rms(mix, nw_ref[3:4, :])


def _mixout(h, conv_o, hgrn_o, sb_o, nw, w_out, l):
    m, d = h.shape
    tm = min(_row_tile(m), 512)
    dc, dh, ds = conv_o.shape[1], hgrn_o.shape[1], sb_o.shape[1]
    return pl.pallas_call(
        functools.partial(_mixout_kernel, dc=dc, dh=dh),
        out_shape=jax.ShapeDtypeStruct((m, d), F32),
        grid=(m // tm,),
        in_specs=[
            pl.BlockSpec((tm, d), lambda r: (r, 0)),
            pl.BlockSpec((tm, dc), lambda r: (r, 0)),
            pl.BlockSpec((tm, dh), lambda r: (r, 0)),
            pl.BlockSpec((tm, ds), lambda r: (r, 0)),
            pl.BlockSpec((None,) + nw.shape[1:], lambda r: (l, 0, 0)),
            pl.BlockSpec((None,) + w_out.shape[1:], lambda r: (l, 0, 0)),
        ],
        out_specs=pl.BlockSpec((tm, d), lambda r: (r, 0)),
        compiler_params=_cparams("parallel"),
        name="mixout",
    )(h, conv_o, hgrn_o, sb_o, nw, w_out)


def _ple_kernel(h_ref, p_ref, nw_ref, wu_ref, wg_ref, o_ref):
    h = h_ref[...]
    up = _dot(p_ref[...].astype(BF16), wu_ref[...])
    gate = _sigmoid(_dot(h.astype(BF16), wg_ref[...]))
    o_ref[...] = h + _rms(up * gate, nw_ref[6:7, :])


def _ple(h, p, nw, w_up, w_gate, l):
    m, d = h.shape
    dp = p.shape[-1]
    tm = min(_row_tile(m), 512)
    return pl.pallas_call(
        _ple_kernel,
        out_shape=jax.ShapeDtypeStruct((m, d), F32),
        grid=(m // tm,),
        in_specs=[
            pl.BlockSpec((tm, d), lambda r: (r, 0)),
            pl.BlockSpec((None, tm, dp), lambda r: (l, r, 0)),
            pl.BlockSpec((None,) + nw.shape[1:], lambda r: (l, 0, 0)),
            pl.BlockSpec((None,) + w_up.shape[1:], lambda r: (l, 0, 0)),
            pl.BlockSpec((None,) + w_gate.shape[1:], lambda r: (l, 0, 0)),
        ],
        out_specs=pl.BlockSpec((tm, d), lambda r: (r, 0)),
        compiler_params=_cparams("parallel"),
        name="ple",
    )(h, p, nw, w_up, w_gate)


def _layer(h, p_all, l, bsz, t, conv_prev, hgrn_prev, paged, prm):
    nw = prm["norm_w"]
    d_conv = prm["conv_w"].shape[-1]
    d_hgrn = prm["hgrn_norm_w"].shape[-1]
    d_sb = prm["sb_norm_w"].shape[-1]
    h = _ffn(h, nw, prm["ffn_w_gate"], prm["ffn_w_up"], prm["ffn_w_down"], l, 0, 0, 1)
    ab, hg, sq, sk, sv = _proj(h, nw, prm["w_in"], l, (2 * d_conv, 4 * d_hgrn, d_sb, d_sb, d_sb))

    conv_o, conv_state = _conv(ab.reshape(bsz, t, 2 * d_conv), conv_prev, prm["conv_w"], prm["conv_b"],
                               prm["conv_ln_w"], prm["conv_ln_b"], l)

    hg = hg.reshape(bsz, t, 4 * d_hgrn)
    tp = -(-t // HGRN_CHUNK) * HGRN_CHUNK
    if tp != t:
        hg = jnp.pad(hg, ((0, 0), (0, tp - t), (0, 0)))
    hgrn_o, hgrn_state = _hgrn(hg, hgrn_prev, prm["hgrn_lb_logits"], prm["hgrn_norm_w"], l, t)
    hgrn_o = hgrn_o[:, :t]

    q3, k3, v3 = (a.reshape(bsz, t, d_sb) for a in (sq, sk, sv))
    if paged is None:
        sb_o = _sb_prompt(q3, k3, v3, prm["sb_bias"], prm["sb_norm_w"], l)
    else:
        cache_k, cache_v, page_table = paged
        tq = -(-t // 8) * 8
        qpad = jnp.pad(q3, ((0, 0), (0, tq - t), (0, 0)))
        kpad = jnp.pad(k3, ((0, 0), (0, SB_BLOCK - t), (0, 0)))
        vpad = jnp.pad(v3, ((0, 0), (0, SB_BLOCK - t), (0, 0)))
        sb_o = _sb_sample(qpad, kpad, vpad, cache_k, cache_v, page_table, prm["sb_bias"],
                          prm["sb_norm_w"], l)[:, :t]

    h = _mixout(h, conv_o.reshape(bsz * t, d_conv), hgrn_o.reshape(bsz * t, d_hgrn),
                sb_o.reshape(bsz * t, d_sb), nw, prm["w_out"], l)
    h = _ffn(h, nw, prm["ffn_w_gate"], prm["ffn_w_up"], prm["ffn_w_down"], l, 1, 4, 5)
    h = _ple(h, p_all, nw, prm["ple_w_up"], prm["ple_w_gate"], l)
    return h, (sk, sv, conv_state, hgrn_state)


def kernel(x_prompt, x_sample, cache_k, cache_v, state_conv, state_hgrn, page_table, p_prompt, p_sample, norm_w, ffn_w_gate, ffn_w_up, ffn_w_down, w_in, conv_w, conv_b, conv_ln_w, conv_ln_b, hgrn_lb_logits, hgrn_norm_w, sb_bias, sb_norm_w, w_out, ple_w_up, ple_w_gate):
    depth = w_in.shape[0]
    bp, tpr, dm = x_prompt.shape
    bs, ts, _ = x_sample.shape
    n_sb = sb_bias.shape[1]
    n_hg = state_hgrn.shape[2]
    d_conv = conv_w.shape[-1]
    prm = dict(norm_w=norm_w, ffn_w_gate=ffn_w_gate.astype(BF16), ffn_w_up=ffn_w_up.astype(BF16),
               ffn_w_down=ffn_w_down.astype(BF16), w_in=w_in.astype(BF16), conv_w=conv_w, conv_b=conv_b,
               conv_ln_w=conv_ln_w, conv_ln_b=conv_ln_b, hgrn_lb_logits=hgrn_lb_logits,
               hgrn_norm_w=hgrn_norm_w, sb_bias=sb_bias, sb_norm_w=sb_norm_w, w_out=w_out.astype(BF16),
               ple_w_up=ple_w_up.astype(BF16), ple_w_gate=ple_w_gate.astype(BF16))

    h_p = x_prompt.reshape(bp * tpr, dm)
    h_s = x_sample.reshape(bs * ts, dm)
    pp = p_prompt.reshape(depth, bp * tpr, -1)
    ps = p_sample.reshape(depth, bs * ts, -1)
    conv0 = jnp.zeros((bp, CONV_WIDTH - 1, d_conv), F32)
    hgrn0 = jnp.zeros((bp, n_hg // 2, LANES, LANES), F32)
    outs = [[] for _ in range(8)]
    for l in range(depth):
        h_p, st_p = _layer(h_p, pp, l, bp, tpr, conv0, hgrn0, None, prm)
        h_s, st_s = _layer(h_s, ps, l, bs, ts, state_conv[l], _pack_state(state_hgrn[l]),
                           (cache_k, cache_v, page_table), prm)
        for i, a in enumerate(st_p + st_s):
            outs[i].append(a)

    def heads(xs, b, t, n):
        return jnp.stack(xs).reshape(depth, b, t, n, HEAD_DIM)

    return (h_p.reshape(bp, tpr, dm), h_s.reshape(bs, ts, dm),
            heads(outs[0], bp, tpr, n_sb), heads(outs[1], bp, tpr, n_sb),
            jnp.stack(outs[2]), jnp.stack([_unpack_state(s) for s in outs[3]]),
            heads(outs[4], bs, ts, n_sb), heads(outs[5], bs, ts, n_sb),
            jnp.stack(outs[6]), jnp.stack([_unpack_state(s) for s in outs[7]]))
```

```python
import functools

import numpy as np
import jax
import jax.numpy as jnp
from jax import lax
from jax.experimental import pallas as pl
from jax.experimental.pallas import tpu as pltpu

F32 = jnp.float32
BF16 = jnp.bfloat16

EPS = 1e-6
HEAD_DIM = 64
LANES = 128
CONV_WIDTH = 31
CONV_PAD = 32
HGRN_CHUNK = 64
SB_BLOCK = 128
PAGES_PER_STEP = 8
VMEM_LIMIT = 48 * 1024 * 1024


def _cparams(*sem):
    return pltpu.CompilerParams(dimension_semantics=sem, vmem_limit_bytes=VMEM_LIMIT)


def _rms(x, w):
    ms = jnp.mean(x * x, axis=-1, keepdims=True)
    return x * lax.rsqrt(ms + EPS) * w


def _sigmoid(x):
    return 1.0 / (1.0 + jnp.exp(-x))


def _softplus(x):
    return jnp.maximum(x, 0.0) + jnp.log1p(jnp.exp(-jnp.abs(x)))


def _dot(a, b):
    return jnp.dot(a, b, preferred_element_type=F32)


def _dot_nt(a, b):
    return lax.dot_general(a, b, (((1,), (1,)), ((), ())), preferred_element_type=F32)


def _dot_tn(a, b):
    return lax.dot_general(a, b, (((0,), (0,)), ((), ())), preferred_element_type=F32)


def _split(x):
    hi = x.astype(BF16)
    lo = (x - hi.astype(F32)).astype(BF16)
    return hi, lo


def _dot_split(x, w):
    hi, lo = _split(x)
    return _dot(hi, w) + _dot(lo, w)


def _head_rms(o, g_ref, w):
    ms = _dot_split(o * o, g_ref[...])
    return o * lax.rsqrt(ms + EPS) * w


def _row_tile(m):
    for t in (1024, 512, 256, 128, 64, 32, 16, 8):
        if m % t == 0:
            return t
    raise ValueError(f"token count {m} must be a multiple of 8")


def _ffn_kernel(h_ref, nw_ref, wg_ref, wu_ref, wd_ref, o_ref, xn_ref, acc_ref, *, pre, post, nf):
    f = pl.program_id(1)

    @pl.when(f == 0)
    def _():
        xn_ref[...] = _rms(h_ref[...], nw_ref[pre:pre + 1, :]).astype(BF16)
        acc_ref[...] = jnp.zeros_like(acc_ref)

    xn = xn_ref[...]
    g = _dot(xn, wg_ref[...])
    u = _dot(xn, wu_ref[...])
    a = (g * _sigmoid(g) * u).astype(BF16)
    acc_ref[...] += _dot(a, wd_ref[...])

    @pl.when(f == nf - 1)
    def _():
        o_ref[...] = h_ref[...] + 0.5 * _rms(acc_ref[...], nw_ref[post:post + 1, :])


def _ffn(h, nw, wg, wu, wd, l, i, pre, post):
    m, d = h.shape
    dff = wg.shape[-1]
    tm = _row_tile(m)
    tf = 256
    nf = dff // tf
    return pl.pallas_call(
        functools.partial(_ffn_kernel, pre=pre, post=post, nf=nf),
        out_shape=jax.ShapeDtypeStruct((m, d), F32),
        grid=(m // tm, nf),
        in_specs=[
            pl.BlockSpec((tm, d), lambda r, f: (r, 0)),
            pl.BlockSpec((None,) + nw.shape[1:], lambda r, f: (l, 0, 0)),
            pl.BlockSpec((None, None, d, tf), lambda r, f: (l, i, 0, f)),
            pl.BlockSpec((None, None, d, tf), lambda r, f: (l, i, 0, f)),
            pl.BlockSpec((None, None, tf, d), lambda r, f: (l, i, f, 0)),
        ],
        out_specs=pl.BlockSpec((tm, d), lambda r, f: (r, 0)),
        scratch_shapes=[pltpu.VMEM((tm, d), BF16), pltpu.VMEM((tm, d), F32)],
        compiler_params=_cparams("parallel", "arbitrary"),
        name="ffn",
    )(h, nw, wg, wu, wd)


def _proj_kernel(h_ref, nw_ref, w_ref, conv_ref, hg_ref, q_ref, k_ref, v_ref, *, bounds):
    xn = _rms(h_ref[...], nw_ref[2:3, :]).astype(BF16)
    for ref, (lo, hi) in zip((conv_ref, hg_ref, q_ref, k_ref, v_ref), bounds):
        ref[...] = _dot(xn, w_ref[:, lo:hi])


def _proj(h, nw, w_in, l, widths):
    m, d = h.shape
    tm = min(_row_tile(m), 512)
    bounds, off = [], 0
    for w in widths:
        bounds.append((off, off + w))
        off += w
    return pl.pallas_call(
        functools.partial(_proj_kernel, bounds=tuple(bounds)),
        out_shape=[jax.ShapeDtypeStruct((m, w), F32) for w in widths],
        grid=(m // tm,),
        in_specs=[
            pl.BlockSpec((tm, d), lambda r: (r, 0)),
            pl.BlockSpec((None,) + nw.shape[1:], lambda r: (l, 0, 0)),
            pl.BlockSpec((None,) + w_in.shape[1:], lambda r: (l, 0, 0)),
        ],
        out_specs=[pl.BlockSpec((tm, w), lambda r: (r, 0)) for w in widths],
        compiler_params=_cparams("parallel"),
        name="proj",
    )(h, nw, w_in)


def _conv_kernel(ab_ref, prev_ref, w_ref, b_ref, lnw_ref, lnb_ref, y_ref, st_ref, up_ref, *, t, rt, dc):
    npad = CONV_WIDTH - 1
    first = CONV_PAD - npad
    a = ab_ref[:, 0:dc]
    b = ab_ref[:, dc:2 * dc]
    up_ref[0:first, :] = jnp.zeros((first, dc), F32)
    up_ref[first:CONV_PAD, :] = prev_ref[...]
    up_ref[CONV_PAD:CONV_PAD + t, :] = a * _sigmoid(b)
    st_ref[...] = up_ref[first + t:first + t + npad, :]
    for r in range(t // rt):
        base = r * rt + first
        acc = jnp.zeros((rt, dc), F32)
        for j in range(CONV_WIDTH):
            acc = acc + w_ref[j:j + 1, :] * up_ref[base + j:base + j + rt, :]
        y = acc + b_ref[...]
        mu = jnp.mean(y, axis=-1, keepdims=True)
        yc = y - mu
        var = jnp.mean(yc * yc, axis=-1, keepdims=True)
        yn = yc * lax.rsqrt(var + EPS) * lnw_ref[...] + lnb_ref[...]
        y_ref[r * rt:(r + 1) * rt, :] = yn * _sigmoid(yn)


def _conv(ab, prev, w, b, lnw, lnb, l):
    bsz, t, dc2 = ab.shape
    dc = dc2 // 2
    rt = min(t, 256)
    vec = lambda a: pl.BlockSpec((None, 1, dc), lambda s: (l, 0, 0))
    return pl.pallas_call(
        functools.partial(_conv_kernel, t=t, rt=rt, dc=dc),
        out_shape=[jax.ShapeDtypeStruct((bsz, t, dc), F32),
                   jax.ShapeDtypeStruct((bsz, CONV_WIDTH - 1, dc), F32)],
        grid=(bsz,),
        in_specs=[
            pl.BlockSpec((None, t, dc2), lambda s: (s, 0, 0)),
            pl.BlockSpec((None, CONV_WIDTH - 1, dc), lambda s: (s, 0, 0)),
            pl.BlockSpec((None, CONV_WIDTH, dc), lambda s: (l, 0, 0)),
            vec(b), vec(lnw), vec(lnb),
        ],
        out_specs=[pl.BlockSpec((None, t, dc), lambda s: (s, 0, 0)),
                   pl.BlockSpec((None, CONV_WIDTH - 1, dc), lambda s: (s, 0, 0))],
        scratch_shapes=[pltpu.VMEM((CONV_PAD + t, dc), F32)],
        compiler_params=_cparams("parallel"),
        name="conv",
    )(ab, prev, w, b.reshape(b.shape[0], 1, dc), lnw.reshape(lnw.shape[0], 1, dc),
      lnb.reshape(lnb.shape[0], 1, dc))


def _hgrn_consts(c):
    levels = []
    m = c // 2
    while m >= 1:
        levels.append(m)
        m //= 2
    idx = np.arange(c)
    mats = [(idx[None, :] <= idx[:, None]).astype(np.float32)]
    masks = []
    for m in levels:
        blk, half = idx // (2 * m), (idx // m) % 2
        same = blk[:, None] == blk[None, :]
        upper = same & (half[:, None] == 1) & (half[None, :] == 1) & (idx[None, :] <= idx[:, None])
        lower = same & (half[:, None] == 0) & (half[None, :] == 0) & (idx[None, :] > idx[:, None])
        mats += [upper.astype(np.float32), lower.astype(np.float32)]
        masks.append((same & (half[:, None] == 1) & (half[None, :] == 0)).astype(np.float32))
    masks.append(np.eye(c, dtype=np.float32))
    masks = np.stack(masks)
    masks = np.concatenate([masks, masks], axis=1)
    return len(levels), np.concatenate(mats, axis=0), masks


def _head_avg(d):
    i = np.arange(d) // HEAD_DIM
    return (i[:, None] == i[None, :]).astype(np.float32) / HEAD_DIM


def _hgrn_kernel(hg_ref, s0_ref, lbl_ref, nw_ref, mall_ref, mask_ref, g_ref, o_ref, sT_ref, st_ref,
                 *, l, c, cps, d, nlev, nsteps, t_valid):
    step = pl.program_id(1)
    npair = d // LANES

    @pl.when(step == 0)
    def _():
        st_ref[...] = s0_ref[...]

    lg = lbl_ref[...]
    e = jnp.exp(lg - jnp.max(lg, axis=0, keepdims=True))
    sm = e / jnp.sum(e, axis=0, keepdims=True)
    row = lax.broadcasted_iota(jnp.int32, sm.shape, 0)
    lb = jnp.sum(jnp.where((row >= 1) & (row <= l), sm, 0.0), axis=0, keepdims=True)
    log_lb = jnp.log(lb)
    log_1mlb = jnp.log1p(-lb)

    ones16 = jnp.ones((2 * c, LANES), BF16)
    lane = lax.broadcasted_iota(jnp.int32, (c, LANES), 1)
    first = lane < HEAD_DIM
    ri = lax.broadcasted_iota(jnp.int32, (LANES, LANES), 0) // HEAD_DIM
    cj = lax.broadcasted_iota(jnp.int32, (LANES, LANES), 1) // HEAD_DIM
    blockdiag = ri == cj
    mall = mall_ref[...]

    for cc in range(cps):
        rows = slice(cc * c, (cc + 1) * c)
        q = hg_ref[rows, 0:d]
        zf = hg_ref[rows, d:2 * d]
        v = hg_ref[rows, 2 * d:3 * d]
        gate = hg_ref[rows, 3 * d:4 * d]

        bb = log_1mlb - _softplus(-zf)
        log_f = jnp.maximum(log_lb, bb) + jnp.log1p(jnp.exp(-jnp.abs(log_lb - bb)))
        kk = (1.0 - lb) * _sigmoid(-zf)
        if t_valid < nsteps * cps * c:
            trow = lax.broadcasted_iota(jnp.int32, (c, d), 0) + (step * cps + cc) * c
            live = trow < t_valid
            log_f = jnp.where(live, log_f, 0.0)
            kk = jnp.where(live, kk, 0.0)

        lf2 = jnp.concatenate(_split(log_f), axis=0)
        e_all = _dot(mall, lf2)
        bcum = e_all[0:c]
        b_last = bcum[c - 1:c, :]
        qe = (q * jnp.exp(bcum)).astype(BF16)
        kd = (kk * jnp.exp(b_last - bcum)).astype(BF16)
        v16 = v.astype(BF16)

        outs = []
        for p in range(npair):
            sl = slice(p * LANES, (p + 1) * LANES)
            qp, kp = q[:, sl], kk[:, sl]
            amat = jnp.zeros((2 * c, c), F32)
            for i in range(nlev + 1):
                if i < nlev:
                    ql = qp * jnp.exp(e_all[(1 + 2 * i) * c:(2 + 2 * i) * c, sl])
                    kl = kp * jnp.exp(e_all[(2 + 2 * i) * c:(3 + 2 * i) * c, sl])
                else:
                    ql, kl = qp, kp
                qst = jnp.concatenate([jnp.where(first, ql, 0.0), jnp.where(first, 0.0, ql)], axis=0)
                amat = amat + mask_ref[i] * _dot_nt(qst.astype(BF16), kl.astype(BF16))
            a16 = amat.astype(BF16)
            vp = v16[:, sl]
            zero = jnp.zeros_like(vp)
            state = st_ref[p]
            o = _dot(qe[:, sl], state.astype(BF16))
            o = o + _dot(a16[0:c], jnp.where(first, vp, zero)) + _dot(a16[c:2 * c], jnp.where(first, zero, vp))
            outs.append(o)
            elog = _dot_tn(lf2[:, sl], ones16)
            upd = _dot_tn(kd[:, sl], vp)
            st_ref[p] = jnp.exp(elog) * state + jnp.where(blockdiag, upd, 0.0)

        o = jnp.concatenate(outs, axis=1)
        o_ref[rows, :] = _head_rms(o, g_ref, nw_ref[...]) * (gate * _sigmoid(gate))

    @pl.when(step == nsteps - 1)
    def _():
        sT_ref[...] = st_ref[...]


def _hgrn(hg, s0p, lb_logits, norm_w, l, t_valid):
    bsz, t, d4 = hg.shape
    d = d4 // 4
    c = HGRN_CHUNK
    cps = 2 if t % (2 * c) == 0 else 1
    nsteps = t // (c * cps)
    nlev, mall, masks = _hgrn_consts(c)
    npair = d // LANES
    full = lambda a: pl.BlockSpec(a.shape, lambda s, ci: (0,) * a.ndim)
    mall = jnp.asarray(np.concatenate([mall, mall], axis=1), BF16)
    masks = jnp.asarray(masks, F32)
    gavg = jnp.asarray(_head_avg(d), BF16)
    return pl.pallas_call(
        functools.partial(_hgrn_kernel, l=l, c=c, cps=cps, d=d, nlev=nlev, nsteps=nsteps, t_valid=t_valid),
        out_shape=[jax.ShapeDtypeStruct((bsz, t, d), F32),
                   jax.ShapeDtypeStruct((bsz, npair, LANES, LANES), F32)],
        grid=(bsz, nsteps),
        in_specs=[
            pl.BlockSpec((None, c * cps, d4), lambda s, ci: (s, ci, 0)),
            pl.BlockSpec((None, npair, LANES, LANES), lambda s, ci: (s, 0, 0, 0)),
            full(lb_logits),
            pl.BlockSpec((None, 1, d), lambda s, ci: (l, 0, 0)),
            full(mall), full(masks), full(gavg),
        ],
        out_specs=[pl.BlockSpec((None, c * cps, d), lambda s, ci: (s, ci, 0)),
                   pl.BlockSpec((None, npair, LANES, LANES), lambda s, ci: (s, 0, 0, 0))],
        scratch_shapes=[pltpu.VMEM((npair, LANES, LANES), F32)],
        compiler_params=_cparams("parallel", "arbitrary"),
        name="hgrn",
    )(hg, s0p, lb_logits, norm_w.reshape(norm_w.shape[0], 1, d), mall, masks, gavg)


def _pack_state(s):
    bsz, h = s.shape[:2]
    s = s.reshape(bsz, h // 2, 2, HEAD_DIM, HEAD_DIM)
    z = jnp.zeros_like(s[:, :, 0])
    top = jnp.concatenate([s[:, :, 0], z], axis=-1)
    bot = jnp.concatenate([z, s[:, :, 1]], axis=-1)
    return jnp.concatenate([top, bot], axis=-2)


def _unpack_state(sp):
    a = sp[:, :, :HEAD_DIM, :HEAD_DIM]
    b = sp[:, :, HEAD_DIM:, HEAD_DIM:]
    bsz, npair = sp.shape[:2]
    return jnp.stack([a, b], axis=2).reshape(bsz, 2 * npair, HEAD_DIM, HEAD_DIM)


LOG2E = 1.4426950408889634


def _sb_consts():
    j = np.arange(SB_BLOCK)
    after = (j[:, None] > j[None, :]).astype(np.float32)
    uo = np.concatenate([after, np.ones_like(after)], axis=1)
    return np.concatenate([uo, uo], axis=0)


def _softplus2(z):
    return jnp.maximum(z, 0.0) + jnp.log2(1.0 + jnp.exp2(-jnp.abs(z)))


def _sb_stage(chains, masks, carries, uo_ref):
    r = chains[0][0].shape[0]
    t1s, lhs = [], []
    for zs, ms in zip(chains, masks):
        for z, m in zip(zs, ms):
            sp = _softplus2(z)
            t1s.append(z - sp)
            if m is not None:
                sp = jnp.where(m, sp, 0.0)
            hi, lo = _split(sp)
            lhs.append(jnp.concatenate([hi, lo], axis=1))
    both = _dot(lhs[0] if len(lhs) == 1 else jnp.concatenate(lhs, axis=0), uo_ref[...])
    out, new_carries, n = [], [], 0
    for zs, ms, c in zip(chains, masks, carries):
        ws = []
        for m in ms:
            b = both[n * r:(n + 1) * r]
            a = jnp.exp2(t1s[n] - (c + b[:, 0:SB_BLOCK]))
            if m is not None:
                a = jnp.where(m, a, 0.0)
            ws.append(a.astype(BF16))
            c = c + b[:, SB_BLOCK:2 * SB_BLOCK]
            n += 1
        out.append(ws)
        new_carries.append(c)
    return out, new_carries


def _sbp_kernel(bias_ref, q_ref, k_ref, v_ref, nw_ref, uo_ref, g_ref, o_ref,
                k16_ref, ve_ref, vo_ref, qm_ref, acc_ref, car_ref, *, l, d):
    qi = pl.program_id(1)
    blk = SB_BLOCK
    npair = d // LANES
    nh = 2 * npair
    lane_d = lax.broadcasted_iota(jnp.int32, (1, d), 1)
    even = (lane_d // HEAD_DIM) % 2 == 0

    @pl.when(qi == 0)
    def _():
        k16_ref[...] = k_ref[...].astype(BF16)
        vv = v_ref[...]
        ve_ref[...] = jnp.where(even, vv, 0.0).astype(BF16)
        vo_ref[...] = jnp.where(even, 0.0, vv).astype(BF16)

    qs = q_ref[...] * (HEAD_DIM ** -0.5 * LOG2E)
    for p in range(npair):
        sl = slice(p * LANES, (p + 1) * LANES)
        qm_ref[2 * p] = jnp.where(even[:, sl], qs[:, sl], 0.0).astype(BF16)
        qm_ref[2 * p + 1] = jnp.where(even[:, sl], 0.0, qs[:, sl]).astype(BF16)

    def tiles(k0s, mask, carries):
        masks = [mask] + [None] * (len(k0s) - 1)
        chains = []
        for h in range(nh):
            sl = slice((h // 2) * LANES, (h // 2 + 1) * LANES)
            chains.append([_dot_nt(qm_ref[h], k16_ref[pl.ds(k0, blk), sl]) + bias_ref[l, h] * LOG2E for k0 in k0s])
        ws, cs = _sb_stage(chains, [masks] * nh, carries, uo_ref)
        outs = []
        for p in range(npair):
            sl = slice(p * LANES, (p + 1) * LANES)
            o = None
            for i, k0 in enumerate(k0s):
                t = _dot(ws[2 * p][i], ve_ref[pl.ds(k0, blk), sl]) + _dot(ws[2 * p + 1][i], vo_ref[pl.ds(k0, blk), sl])
                o = t if o is None else o + t
            outs.append(o)
        return outs, cs

    def accumulate(k0s):
        outs, cs = tiles(k0s, None, [car_ref[h] for h in range(nh)])
        for p in range(npair):
            acc_ref[:, p * LANES:(p + 1) * LANES] += outs[p]
        for h in range(nh):
            car_ref[h] = cs[h]

    ti = lax.broadcasted_iota(jnp.int32, (blk, blk), 0)
    si = lax.broadcasted_iota(jnp.int32, (blk, blk), 1)
    zero = jnp.zeros((blk, LANES), F32)
    outs, cs = tiles([pl.multiple_of(qi * blk, blk)], si < ti, [zero] * nh)
    for p in range(npair):
        acc_ref[:, p * LANES:(p + 1) * LANES] = outs[p]
    for h in range(nh):
        car_ref[h] = cs[h]

    def body(j, _):
        k0 = pl.multiple_of((qi - 1 - 2 * j) * blk, blk)
        accumulate([k0, pl.multiple_of(k0 - blk, blk)])
        return 0

    lax.fori_loop(0, qi // 2, body, 0)

    @pl.when(qi % 2 == 1)
    def _():
        accumulate([0])

    o_ref[...] = _head_rms(acc_ref[...], g_ref, nw_ref[...])


def _sb_prompt(q, k, v, bias, norm_w, l):
    bsz, t, d = q.shape
    blk = SB_BLOCK
    nh = d // HEAD_DIM
    uo = jnp.asarray(_sb_consts(), BF16)
    gavg = jnp.asarray(_head_avg(d), BF16)
    full = lambda a: pl.BlockSpec(a.shape, lambda s, qi: (0,) * a.ndim)
    return pl.pallas_call(
        functools.partial(_sbp_kernel, l=l, d=d),
        out_shape=jax.ShapeDtypeStruct((bsz, t, d), F32),
        grid=(bsz, t // blk),
        in_specs=[
            pl.BlockSpec(memory_space=pltpu.SMEM),
            pl.BlockSpec((None, blk, d), lambda s, qi: (s, qi, 0)),
            pl.BlockSpec((None, t, d), lambda s, qi: (s, 0, 0)),
            pl.BlockSpec((None, t, d), lambda s, qi: (s, 0, 0)),
            pl.BlockSpec((None, 1, d), lambda s, qi: (l, 0, 0)),
            full(uo), full(gavg),
        ],
        out_specs=pl.BlockSpec((None, blk, d), lambda s, qi: (s, qi, 0)),
        scratch_shapes=[pltpu.VMEM((t, d), BF16), pltpu.VMEM((t, d), BF16), pltpu.VMEM((t, d), BF16),
                        pltpu.VMEM((nh, blk, LANES), BF16), pltpu.VMEM((blk, d), F32),
                        pltpu.VMEM((nh, blk, LANES), F32)],
        compiler_params=_cparams("parallel", "arbitrary"),
        name="sb_prompt",
    )(bias, q, k, v, norm_w.reshape(norm_w.shape[0], 1, d), uo, gavg)


def _sbs_kernel(pt_ref, bias_ref, q_ref, kn_ref, vn_ref, nw_ref, uo_ref, g_ref, *rest, l, d, nh, tq, nstep):
    npg = PAGES_PER_STEP
    kp_refs = rest[0:npg]
    vp_refs = rest[npg:2 * npg]
    o_ref, qbd_ref, acc_ref, car_ref = rest[2 * npg:]
    j = pl.program_id(1)
    rows = nh * tq
    blk = SB_BLOCK
    lane_d = lax.broadcasted_iota(jnp.int32, (rows, d), 1) // HEAD_DIM
    rgrp = lax.broadcasted_iota(jnp.int32, (rows, d), 0) // tq
    own = lane_d == rgrp

    rg1 = lax.broadcasted_iota(jnp.int32, (rows, blk), 0) // tq
    bias = jnp.zeros((rows, blk), F32)
    for h in range(nh):
        bias = jnp.where(rg1 == h, bias_ref[l, h] * LOG2E, bias)

    @pl.when(j == 0)
    def _():
        qs = q_ref[...] * (HEAD_DIM ** -0.5 * LOG2E)
        qbd = jnp.where(own, jnp.concatenate([qs] * nh, axis=0), 0.0).astype(BF16)
        qbd_ref[...] = qbd
        tpos = lax.broadcasted_iota(jnp.int32, (rows, blk), 0) % tq
        spos = lax.broadcasted_iota(jnp.int32, (rows, blk), 1)
        z = _dot_nt(qbd, kn_ref[...].astype(BF16)) + bias
        ws, cs = _sb_stage([[z]], [[spos < tpos]], [jnp.zeros((rows, blk), F32)], uo_ref)
        acc_ref[...] = _dot(ws[0][0], vn_ref[...].astype(BF16))
        car_ref[...] = cs[0]

    qbd = qbd_ref[...]
    zs = [_dot(qbd, kp_refs[i][...].reshape(d, blk).astype(BF16)) + bias for i in range(npg)]
    ws, cs = _sb_stage([zs], [[None] * npg], [car_ref[...]], uo_ref)
    acc = acc_ref[...]
    for i in range(npg):
        acc = acc + _dot_nt(ws[0][i], vp_refs[i][...].reshape(d, blk).astype(BF16))
    acc_ref[...] = acc
    car_ref[...] = cs[0]

    @pl.when(j == nstep - 1)
    def _():
        full = jnp.where(own, acc_ref[...], 0.0)
        o = full[0:tq]
        for h in range(1, nh):
            o = o + full[h * tq:(h + 1) * tq]
        o_ref[...] = _head_rms(o, g_ref, nw_ref[...])


def _sb_sample(q, k_new, v_new, cache_kt, cache_vt, page_table, bias, norm_w, l):
    bsz, tq, d = q.shape
    nh = d // HEAD_DIM
    npages = page_table.shape[1]
    npg = PAGES_PER_STEP
    nstep = npages // npg
    blk = SB_BLOCK
    uo = jnp.asarray(_sb_consts(), BF16)
    gavg = jnp.asarray(_head_avg(d), BF16)
    full = lambda a: pl.BlockSpec(a.shape, lambda s, j, pt: (0,) * a.ndim)

    def page_spec(i):
        return pl.BlockSpec((None, None, nh, HEAD_DIM, blk),
                            lambda s, j, pt: (l, pt[s, npages - 1 - (j * npg + i)], 0, 0, 0))

    grid_spec = pltpu.PrefetchScalarGridSpec(
        num_scalar_prefetch=1,
        grid=(bsz, nstep),
        in_specs=[
            pl.BlockSpec(memory_space=pltpu.SMEM),
            pl.BlockSpec((None, tq, d), lambda s, j, pt: (s, 0, 0)),
            pl.BlockSpec((None, blk, d), lambda s, j, pt: (s, 0, 0)),
            pl.BlockSpec((None, blk, d), lambda s, j, pt: (s, 0, 0)),
            pl.BlockSpec((None, 1, d), lambda s, j, pt: (l, 0, 0)),
            full(uo), full(gavg),
        ] + [page_spec(i) for i in range(npg)] + [page_spec(i) for i in range(npg)],
        out_specs=pl.BlockSpec((None, tq, d), lambda s, j, pt: (s, 0, 0)),
        scratch_shapes=[pltpu.VMEM((nh * tq, d), BF16), pltpu.VMEM((nh * tq, d), F32),
                        pltpu.VMEM((nh * tq, blk), F32)],
    )
    return pl.pallas_call(
        functools.partial(_sbs_kernel, l=l, d=d, nh=nh, tq=tq, nstep=nstep),
        out_shape=jax.ShapeDtypeStruct((bsz, tq, d), F32),
        grid_spec=grid_spec,
        compiler_params=_cparams("parallel", "arbitrary"),
        name="sb_sample",
    )(page_table, bias, q, k_new, v_new, norm_w.reshape(norm_w.shape[0], 1, d), uo, gavg,
      *([cache_kt] * npg), *([cache_vt] * npg))


def _mixout_kernel(h_ref, c_ref, g_ref, s_ref, nw_ref, w_ref, o_ref, *, dc, dh):
    mix = _dot(c_ref[...].astype(BF16), w_ref[0:dc, :])
    mix = mix + _dot(g_ref[...].astype(BF16), w_ref[dc:dc + dh, :])
    mix = mix + _dot(s_ref[...].astype(BF16), w_ref[dc + dh:, :])
    o_ref[...] = h_ref[...] + _rms(mix, nw_ref[3:4, :])


def _mixout(h, conv_o, hgrn_o, sb_o, nw, w_out, l):
    m, d = h.shape
    tm = min(_row_tile(m), 512)
    dc, dh, ds = conv_o.shape[1], hgrn_o.shape[1], sb_o.shape[1]
    return pl.pallas_call(
        functools.partial(_mixout_kernel, dc=dc, dh=dh),
        out_shape=jax.ShapeDtypeStruct((m, d), F32),
        grid=(m // tm,),
        in_specs=[
            pl.BlockSpec((tm, d), lambda r: (r, 0)),
            pl.BlockSpec((tm, dc), lambda r: (r, 0)),
            pl.BlockSpec((tm, dh), lambda r: (r, 0)),
            pl.BlockSpec((tm, ds), lambda r: (r, 0)),
            pl.BlockSpec((None,) + nw.shape[1:], lambda r: (l, 0, 0)),
            pl.BlockSpec((None,) + w_out.shape[1:], lambda r: (l, 0, 0)),
        ],
        out_specs=pl.BlockSpec((tm, d), lambda r: (r, 0)),
        compiler_params=_cparams("parallel"),
        name="mixout",
    )(h, conv_o, hgrn_o, sb_o, nw, w_out)


def _ple_kernel(h_ref, p_ref, nw_ref, wu_ref, wg_ref, o_ref):
    h = h_ref[...]
    up = _dot(p_ref[...].astype(BF16), wu_ref[...])
    gate = _sigmoid(_dot(h.astype(BF16), wg_ref[...]))
    o_ref[...] = h + _rms(up * gate, nw_ref[6:7, :])


def _ple(h, p, nw, w_up, w_gate, l):
    m, d = h.shape
    dp = p.shape[-1]
    tm = min(_row_tile(m), 512)
    return pl.pallas_call(
        _ple_kernel,
        out_shape=jax.ShapeDtypeStruct((m, d), F32),
        grid=(m // tm,),
        in_specs=[
            pl.BlockSpec((tm, d), lambda r: (r, 0)),
            pl.BlockSpec((None, tm, dp), lambda r: (l, r, 0)),
            pl.BlockSpec((None,) + nw.shape[1:], lambda r: (l, 0, 0)),
            pl.BlockSpec((None,) + w_up.shape[1:], lambda r: (l, 0, 0)),
            pl.BlockSpec((None,) + w_gate.shape[1:], lambda r: (l, 0, 0)),
        ],
        out_specs=pl.BlockSpec((tm, d), lambda r: (r, 0)),
        compiler_params=_cparams("parallel"),
        name="ple",
    )(h, p, nw, w_up, w_gate)


def _layer(h, p_all, l, bsz, t, conv_prev, hgrn_prev, paged, prm):
    nw = prm["norm_w"]
    d_conv = prm["conv_w"].shape[-1]
    d_hgrn = prm["hgrn_norm_w"].shape[-1]
    d_sb = prm["sb_norm_w"].shape[-1]
    h = _ffn(h, nw, prm["ffn_w_gate"], prm["ffn_w_up"], prm["ffn_w_down"], l, 0, 0, 1)
    ab, hg, sq, sk, sv = _proj(h, nw, prm["w_in"], l, (2 * d_conv, 4 * d_hgrn, d_sb, d_sb, d_sb))

    conv_o, conv_state = _conv(ab.reshape(bsz, t, 2 * d_conv), conv_prev, prm["conv_w"], prm["conv_b"],
                               prm["conv_ln_w"], prm["conv_ln_b"], l)

    hg = hg.reshape(bsz, t, 4 * d_hgrn)
    tp = -(-t // HGRN_CHUNK) * HGRN_CHUNK
    if tp != t:
        hg = jnp.pad(hg, ((0, 0), (0, tp - t), (0, 0)))
    hgrn_o, hgrn_state = _hgrn(hg, hgrn_prev, prm["hgrn_lb_logits"], prm["hgrn_norm_w"], l, t)
    hgrn_o = hgrn_o[:, :t]

    q3, k3, v3 = (a.reshape(bsz, t, d_sb) for a in (sq, sk, sv))
    if paged is None:
        sb_o = _sb_prompt(q3, k3, v3, prm["sb_bias"], prm["sb_norm_w"], l)
    else:
        cache_kt, cache_vt, page_table = paged
        tq = -(-t // 8) * 8
        qpad = jnp.pad(q3, ((0, 0), (0, tq - t), (0, 0)))
        kpad = jnp.pad(k3, ((0, 0), (0, SB_BLOCK - t), (0, 0)))
        vpad = jnp.pad(v3, ((0, 0), (0, SB_BLOCK - t), (0, 0)))
        sb_o = _sb_sample(qpad, kpad, vpad, cache_kt, cache_vt, page_table, prm["sb_bias"],
                          prm["sb_norm_w"], l)[:, :t]

    h = _mixout(h, conv_o.reshape(bsz * t, d_conv), hgrn_o.reshape(bsz * t, d_hgrn),
                sb_o.reshape(bsz * t, d_sb), nw, prm["w_out"], l)
    h = _ffn(h, nw, prm["ffn_w_gate"], prm["ffn_w_up"], prm["ffn_w_down"], l, 1, 4, 5)
    h = _ple(h, p_all, nw, prm["ple_w_up"], prm["ple_w_gate"], l)
    return h, (sk, sv, conv_state, hgrn_state)


def kernel(x_prompt, x_sample, cache_k, cache_v, state_conv, state_hgrn, page_table, p_prompt, p_sample, norm_w, ffn_w_gate, ffn_w_up, ffn_w_down, w_in, conv_w, conv_b, conv_ln_w, conv_ln_b, hgrn_lb_logits, hgrn_norm_w, sb_bias, sb_norm_w, w_out, ple_w_up, ple_w_gate):
    depth = w_in.shape[0]
    bp, tpr, dm = x_prompt.shape
    bs, ts, _ = x_sample.shape
    n_sb = sb_bias.shape[1]
    n_hg = state_hgrn.shape[2]
    d_conv = conv_w.shape[-1]
    prm = dict(norm_w=norm_w, ffn_w_gate=ffn_w_gate.astype(BF16), ffn_w_up=ffn_w_up.astype(BF16),
               ffn_w_down=ffn_w_down.astype(BF16), w_in=w_in.astype(BF16), conv_w=conv_w, conv_b=conv_b,
               conv_ln_w=conv_ln_w, conv_ln_b=conv_ln_b, hgrn_lb_logits=hgrn_lb_logits,
               hgrn_norm_w=hgrn_norm_w, sb_bias=sb_bias, sb_norm_w=sb_norm_w, w_out=w_out.astype(BF16),
               ple_w_up=ple_w_up.astype(BF16), ple_w_gate=ple_w_gate.astype(BF16))

    h_p = x_prompt.reshape(bp * tpr, dm)
    h_s = x_sample.reshape(bs * ts, dm)
    pp = p_prompt.reshape(depth, bp * tpr, -1)
    ps = p_sample.reshape(depth, bs * ts, -1)
    conv0 = jnp.zeros((bp, CONV_WIDTH - 1, d_conv), F32)
    hgrn0 = jnp.zeros((bp, n_hg // 2, LANES, LANES), F32)
    cache_kt = jnp.transpose(cache_k, (0, 1, 3, 4, 2))
    cache_vt = jnp.transpose(cache_v, (0, 1, 3, 4, 2))
    outs = [[] for _ in range(8)]
    for l in range(depth):
        h_p, st_p = _layer(h_p, pp, l, bp, tpr, conv0, hgrn0, None, prm)
        h_s, st_s = _layer(h_s, ps, l, bs, ts, state_conv[l], _pack_state(state_hgrn[l]),
                           (cache_kt, cache_vt, page_table), prm)
        for i, a in enumerate(st_p + st_s):
            outs[i].append(a)

    def heads(xs, b, t, n):
        return jnp.stack(xs).reshape(depth, b, t, n, HEAD_DIM)

    return (h_p.reshape(bp, tpr, dm), h_s.reshape(bs, ts, dm),
            heads(outs[0], bp, tpr, n_sb), heads(outs[1], bp, tpr, n_sb),
            jnp.stack(outs[2]), jnp.stack([_unpack_state(s) for s in outs[3]]),
            heads(outs[4], bs, ts, n_sb), heads(outs[5], bs, ts, n_sb),
            jnp.stack(outs[6]), jnp.stack([_unpack_state(s) for s in outs[7]]))
```

```python
import functools

import numpy as np
import jax
import jax.numpy as jnp
from jax import lax
from jax.experimental import pallas as pl
from jax.experimental.pallas import tpu as pltpu

F32 = jnp.float32
BF16 = jnp.bfloat16

EPS = 1e-6
HEAD_DIM = 64
LANES = 128
CONV_WIDTH = 31
CONV_TILE = 64
CONV_UNROLL = 4
CONV_PAD = 32
HGRN_CHUNK = 64
SB_BLOCK = 128
SB_QROWS = 256
PAGES_PER_STEP = 8
VMEM_LIMIT = 48 * 1024 * 1024


def _cparams(*sem):
    return pltpu.CompilerParams(dimension_semantics=sem, vmem_limit_bytes=VMEM_LIMIT)


def _rms(x, w):
    ms = jnp.mean(x * x, axis=-1, keepdims=True)
    return x * lax.rsqrt(ms + EPS) * w


def _sigmoid(x):
    return 1.0 / (1.0 + jnp.exp(-x))


def _softplus(x):
    return jnp.maximum(x, 0.0) + jnp.log1p(jnp.exp(-jnp.abs(x)))


def _dot(a, b):
    return jnp.dot(a, b, preferred_element_type=F32)


def _dot_nt(a, b):
    return lax.dot_general(a, b, (((1,), (1,)), ((), ())), preferred_element_type=F32)


def _dot_tn(a, b):
    return lax.dot_general(a, b, (((0,), (0,)), ((), ())), preferred_element_type=F32)


def _split(x):
    hi = x.astype(BF16)
    lo = (x - hi.astype(F32)).astype(BF16)
    return hi, lo


def _dot_split(x, w):
    hi, lo = _split(x)
    return _dot(hi, w) + _dot(lo, w)


def _head_rms(o, g_ref, w):
    ms = _dot_split(o * o, g_ref[...])
    return o * lax.rsqrt(ms + EPS) * w


def _row_tile(m):
    for t in (1024, 512, 256, 128, 64, 32, 16, 8):
        if m % t == 0:
            return t
    raise ValueError(f"token count {m} must be a multiple of 8")


def _ffn_kernel(h_ref, nw_ref, wg_ref, wu_ref, wd_ref, o_ref, xn_ref, acc_ref, *, pre, post, nf):
    f = pl.program_id(1)

    @pl.when(f == 0)
    def _():
        xn_ref[...] = _rms(h_ref[...], nw_ref[pre:pre + 1, :]).astype(BF16)
        acc_ref[...] = jnp.zeros_like(acc_ref)

    xn = xn_ref[...]
    g = _dot(xn, wg_ref[...])
    u = _dot(xn, wu_ref[...])
    a = (g * _sigmoid(g) * u).astype(BF16)
    acc_ref[...] += _dot(a, wd_ref[...])

    @pl.when(f == nf - 1)
    def _():
        o_ref[...] = h_ref[...] + 0.5 * _rms(acc_ref[...], nw_ref[post:post + 1, :])


def _ffn(h, nw, wg, wu, wd, l, i, pre, post):
    m, d = h.shape
    dff = wg.shape[-1]
    tm = _row_tile(m)
    tf = 256
    nf = dff // tf
    return pl.pallas_call(
        functools.partial(_ffn_kernel, pre=pre, post=post, nf=nf),
        out_shape=jax.ShapeDtypeStruct((m, d), F32),
        grid=(m // tm, nf),
        in_specs=[
            pl.BlockSpec((tm, d), lambda r, f: (r, 0)),
            pl.BlockSpec((None,) + nw.shape[1:], lambda r, f: (l, 0, 0)),
            pl.BlockSpec((None, None, d, tf), lambda r, f: (l, i, 0, f)),
            pl.BlockSpec((None, None, d, tf), lambda r, f: (l, i, 0, f)),
            pl.BlockSpec((None, None, tf, d), lambda r, f: (l, i, f, 0)),
        ],
        out_specs=pl.BlockSpec((tm, d), lambda r, f: (r, 0)),
        scratch_shapes=[pltpu.VMEM((tm, d), BF16), pltpu.VMEM((tm, d), F32)],
        compiler_params=_cparams("parallel", "arbitrary"),
        name="ffn",
    )(h, nw, wg, wu, wd)


def _proj_kernel(h_ref, nw_ref, w_ref, conv_ref, hg_ref, q_ref, k_ref, v_ref, *, bounds):
    xn = _rms(h_ref[...], nw_ref[2:3, :]).astype(BF16)
    for ref, (lo, hi) in zip((conv_ref, hg_ref, q_ref, k_ref, v_ref), bounds):
        ref[...] = _dot(xn, w_ref[:, lo:hi])


def _proj(h, nw, w_in, l, widths):
    m, d = h.shape
    tm = min(_row_tile(m), 512)
    bounds, off = [], 0
    for w in widths:
        bounds.append((off, off + w))
        off += w
    return pl.pallas_call(
        functools.partial(_proj_kernel, bounds=tuple(bounds)),
        out_shape=[jax.ShapeDtypeStruct((m, w), F32) for w in widths],
        grid=(m // tm,),
        in_specs=[
            pl.BlockSpec((tm, d), lambda r: (r, 0)),
            pl.BlockSpec((None,) + nw.shape[1:], lambda r: (l, 0, 0)),
            pl.BlockSpec((None,) + w_in.shape[1:], lambda r: (l, 0, 0)),
        ],
        out_specs=[pl.BlockSpec((tm, w), lambda r: (r, 0)) for w in widths],
        compiler_params=_cparams("parallel"),
        name="proj",
    )(h, nw, w_in)


def _conv_kernel(ab_ref, prev_ref, w_ref, b_ref, lnw_ref, lnb_ref, y_ref, st_ref, up_ref, *, t, rt, dc):
    npad = CONV_WIDTH - 1
    first = CONV_PAD - npad
    a = ab_ref[:, 0:dc]
    b = ab_ref[:, dc:2 * dc]
    up_ref[0:first, :] = jnp.zeros((first, dc), F32)
    up_ref[first:CONV_PAD, :] = prev_ref[...]
    up_ref[CONV_PAD:CONV_PAD + t, :] = a * _sigmoid(b)
    st_ref[...] = up_ref[first + t:first + t + npad, :]

    groups = [[] for _ in range(8)]
    for j in range(CONV_WIDTH):
        blk8, rem = divmod(j + first, 8)
        groups[rem].append((blk8, j))

    def tile(base):
        y = None
        for rem, taps in enumerate(groups):
            if not taps:
                continue
            n = rt + 8 if rem else rt
            part = None
            for blk8, j in taps:
                term = w_ref[j:j + 1, :] * up_ref[pl.ds(base + 8 * blk8, n), :]
                part = term if part is None else part + term
            if rem:
                part = part[rem:rem + rt]
            y = part if y is None else y + part
        y = y + b_ref[...]
        mu = jnp.mean(y, axis=-1, keepdims=True)
        yc = y - mu
        var = jnp.mean(yc * yc, axis=-1, keepdims=True)
        yn = yc * lax.rsqrt(var + EPS) * lnw_ref[...] + lnb_ref[...]
        y_ref[pl.ds(base, rt), :] = yn * _sigmoid(yn)

    if t == rt:
        tile(0)
    else:
        def body(i, _):
            for u in range(CONV_UNROLL):
                tile(pl.multiple_of((i * CONV_UNROLL + u) * rt, rt))
            return 0
        lax.fori_loop(0, t // (rt * CONV_UNROLL), body, 0)


def _conv(ab, prev, w, b, lnw, lnb, l):
    bsz, t, dc2 = ab.shape
    dc = dc2 // 2
    rt = CONV_TILE if t % (CONV_TILE * CONV_UNROLL) == 0 else t
    vec = lambda a: pl.BlockSpec((None, 1, dc), lambda s: (l, 0, 0))
    return pl.pallas_call(
        functools.partial(_conv_kernel, t=t, rt=rt, dc=dc),
        out_shape=[jax.ShapeDtypeStruct((bsz, t, dc), F32),
                   jax.ShapeDtypeStruct((bsz, CONV_WIDTH - 1, dc), F32)],
        grid=(bsz,),
        in_specs=[
            pl.BlockSpec((None, t, dc2), lambda s: (s, 0, 0)),
            pl.BlockSpec((None, CONV_WIDTH - 1, dc), lambda s: (s, 0, 0)),
            pl.BlockSpec((None, CONV_WIDTH, dc), lambda s: (l, 0, 0)),
            vec(b), vec(lnw), vec(lnb),
        ],
        out_specs=[pl.BlockSpec((None, t, dc), lambda s: (s, 0, 0)),
                   pl.BlockSpec((None, CONV_WIDTH - 1, dc), lambda s: (s, 0, 0))],
        scratch_shapes=[pltpu.VMEM((CONV_PAD + t, dc), F32)],
        compiler_params=_cparams("parallel"),
        name="conv",
    )(ab, prev, w, b.reshape(b.shape[0], 1, dc), lnw.reshape(lnw.shape[0], 1, dc),
      lnb.reshape(lnb.shape[0], 1, dc))


def _hgrn_consts(c):
    levels = []
    m = c // 2
    while m >= 1:
        levels.append(m)
        m //= 2
    idx = np.arange(c)
    mats = [(idx[None, :] <= idx[:, None]).astype(np.float32)]
    masks = []
    for m in levels:
        blk, half = idx // (2 * m), (idx // m) % 2
        same = blk[:, None] == blk[None, :]
        upper = same & (half[:, None] == 1) & (half[None, :] == 1) & (idx[None, :] <= idx[:, None])
        lower = same & (half[:, None] == 0) & (half[None, :] == 0) & (idx[None, :] > idx[:, None])
        mats += [upper.astype(np.float32), lower.astype(np.float32)]
        masks.append((same & (half[:, None] == 1) & (half[None, :] == 0)).astype(np.float32))
    masks.append(np.eye(c, dtype=np.float32))
    masks = np.stack(masks)
    masks = np.concatenate([masks, masks], axis=1)
    return len(levels), np.concatenate(mats, axis=0), masks


def _head_avg(d):
    i = np.arange(d) // HEAD_DIM
    return (i[:, None] == i[None, :]).astype(np.float32) / HEAD_DIM


def _hgrn_kernel(hg_ref, s0_ref, lbl_ref, nw_ref, mall_ref, mask_ref, g_ref, o_ref, sT_ref, st_ref,
                 *, l, c, cps, d, nlev, nsteps, t_valid):
    step = pl.program_id(1)
    npair = d // LANES

    @pl.when(step == 0)
    def _():
        st_ref[...] = s0_ref[...]

    lg = lbl_ref[...]
    e = jnp.exp(lg - jnp.max(lg, axis=0, keepdims=True))
    sm = e / jnp.sum(e, axis=0, keepdims=True)
    row = lax.broadcasted_iota(jnp.int32, sm.shape, 0)
    lb = jnp.sum(jnp.where((row >= 1) & (row <= l), sm, 0.0), axis=0, keepdims=True)
    log_lb = jnp.log(lb)
    log_1mlb = jnp.log1p(-lb)

    ones16 = jnp.ones((2 * c, LANES), BF16)
    lane = lax.broadcasted_iota(jnp.int32, (c, LANES), 1)
    first = lane < HEAD_DIM
    ri = lax.broadcasted_iota(jnp.int32, (LANES, LANES), 0) // HEAD_DIM
    cj = lax.broadcasted_iota(jnp.int32, (LANES, LANES), 1) // HEAD_DIM
    blockdiag = ri == cj
    mall = mall_ref[...]

    for cc in range(cps):
        rows = slice(cc * c, (cc + 1) * c)
        q = hg_ref[rows, 0:d]
        zf = hg_ref[rows, d:2 * d]
        v = hg_ref[rows, 2 * d:3 * d]
        gate = hg_ref[rows, 3 * d:4 * d]

        bb = log_1mlb - _softplus(-zf)
        log_f = jnp.maximum(log_lb, bb) + jnp.log1p(jnp.exp(-jnp.abs(log_lb - bb)))
        kk = (1.0 - lb) * _sigmoid(-zf)
        if t_valid < nsteps * cps * c:
            trow = lax.broadcasted_iota(jnp.int32, (c, d), 0) + (step * cps + cc) * c
            live = trow < t_valid
            log_f = jnp.where(live, log_f, 0.0)
            kk = jnp.where(live, kk, 0.0)

        lf2 = jnp.concatenate(_split(log_f), axis=0)
        e_all = _dot(mall, lf2)
        bcum = e_all[0:c]
        b_last = bcum[c - 1:c, :]
        qe = (q * jnp.exp(bcum)).astype(BF16)
        kd = (kk * jnp.exp(b_last - bcum)).astype(BF16)
        v16 = v.astype(BF16)

        outs = []
        for p in range(npair):
            sl = slice(p * LANES, (p + 1) * LANES)
            qp, kp = q[:, sl], kk[:, sl]
            amat = jnp.zeros((2 * c, c), F32)
            for i in range(nlev + 1):
                if i < nlev:
                    ql = qp * jnp.exp(e_all[(1 + 2 * i) * c:(2 + 2 * i) * c, sl])
                    kl = kp * jnp.exp(e_all[(2 + 2 * i) * c:(3 + 2 * i) * c, sl])
                else:
                    ql, kl = qp, kp
                qst = jnp.concatenate([jnp.where(first, ql, 0.0), jnp.where(first, 0.0, ql)], axis=0)
                amat = amat + mask_ref[i] * _dot_nt(qst.astype(BF16), kl.astype(BF16))
            a16 = amat.astype(BF16)
            vp = v16[:, sl]
            zero = jnp.zeros_like(vp)
            state = st_ref[p]
            o = _dot(qe[:, sl], state.astype(BF16))
            o = o + _dot(a16[0:c], jnp.where(first, vp, zero)) + _dot(a16[c:2 * c], jnp.where(first, zero, vp))
            outs.append(o)
            elog = _dot_tn(lf2[:, sl], ones16)
            upd = _dot_tn(kd[:, sl], vp)
            st_ref[p] = jnp.exp(elog) * state + jnp.where(blockdiag, upd, 0.0)

        o = jnp.concatenate(outs, axis=1)
        o_ref[rows, :] = _head_rms(o, g_ref, nw_ref[...]) * (gate * _sigmoid(gate))

    @pl.when(step == nsteps - 1)
    def _():
        sT_ref[...] = st_ref[...]


def _hgrn(hg, s0p, lb_logits, norm_w, l, t_valid):
    bsz, t, d4 = hg.shape
    d = d4 // 4
    c = HGRN_CHUNK
    cps = 2 if t % (2 * c) == 0 else 1
    nsteps = t // (c * cps)
    nlev, mall, masks = _hgrn_consts(c)
    npair = d // LANES
    full = lambda a: pl.BlockSpec(a.shape, lambda s, ci: (0,) * a.ndim)
    mall = jnp.asarray(np.concatenate([mall, mall], axis=1), BF16)
    masks = jnp.asarray(masks, F32)
    gavg = jnp.asarray(_head_avg(d), BF16)
    return pl.pallas_call(
        functools.partial(_hgrn_kernel, l=l, c=c, cps=cps, d=d, nlev=nlev, nsteps=nsteps, t_valid=t_valid),
        out_shape=[jax.ShapeDtypeStruct((bsz, t, d), F32),
                   jax.ShapeDtypeStruct((bsz, npair, LANES, LANES), F32)],
        grid=(bsz, nsteps),
        in_specs=[
            pl.BlockSpec((None, c * cps, d4), lambda s, ci: (s, ci, 0)),
            pl.BlockSpec((None, npair, LANES, LANES), lambda s, ci: (s, 0, 0, 0)),
            full(lb_logits),
            pl.BlockSpec((None, 1, d), lambda s, ci: (l, 0, 0)),
            full(mall), full(masks), full(gavg),
        ],
        out_specs=[pl.BlockSpec((None, c * cps, d), lambda s, ci: (s, ci, 0)),
                   pl.BlockSpec((None, npair, LANES, LANES), lambda s, ci: (s, 0, 0, 0))],
        scratch_shapes=[pltpu.VMEM((npair, LANES, LANES), F32)],
        compiler_params=_cparams("parallel", "arbitrary"),
        name="hgrn",
    )(hg, s0p, lb_logits, norm_w.reshape(norm_w.shape[0], 1, d), mall, masks, gavg)


def _pack_state(s):
    bsz, h = s.shape[:2]
    s = s.reshape(bsz, h // 2, 2, HEAD_DIM, HEAD_DIM)
    z = jnp.zeros_like(s[:, :, 0])
    top = jnp.concatenate([s[:, :, 0], z], axis=-1)
    bot = jnp.concatenate([z, s[:, :, 1]], axis=-1)
    return jnp.concatenate([top, bot], axis=-2)


def _unpack_state(sp):
    a = sp[:, :, :HEAD_DIM, :HEAD_DIM]
    b = sp[:, :, HEAD_DIM:, HEAD_DIM:]
    bsz, npair = sp.shape[:2]
    return jnp.stack([a, b], axis=2).reshape(bsz, 2 * npair, HEAD_DIM, HEAD_DIM)


LOG2E = 1.4426950408889634


def _sb_consts():
    j = np.arange(SB_BLOCK)
    after = (j[:, None] > j[None, :]).astype(np.float32)
    uo = np.concatenate([after, np.ones_like(after)], axis=1)
    return np.concatenate([uo, uo], axis=0)


def _softplus2(z):
    neg_abs = lax.bitcast_convert_type(lax.bitcast_convert_type(z, jnp.uint32) | jnp.uint32(0x80000000), F32)
    return jnp.maximum(z, 0.0) + jnp.log2(1.0 + jnp.exp2(neg_abs))


def _sb_stage(chains, masks, carries, uo_ref):
    r = chains[0][0].shape[0]
    t1s, lhs = [], []
    for zs, ms in zip(chains, masks):
        for z, m in zip(zs, ms):
            sp = _softplus2(z)
            t1s.append(z - sp)
            if m is not None:
                sp = jnp.where(m, sp, 0.0)
            hi, lo = _split(sp)
            lhs.append(jnp.concatenate([hi, lo], axis=1))
    both = _dot(lhs[0] if len(lhs) == 1 else jnp.concatenate(lhs, axis=0), uo_ref[...])
    out, new_carries, n = [], [], 0
    for zs, ms, c in zip(chains, masks, carries):
        ws = []
        for m in ms:
            b = both[n * r:(n + 1) * r]
            a = jnp.exp2(t1s[n] - (c + b[:, 0:SB_BLOCK]))
            if m is not None:
                a = jnp.where(m, a, 0.0)
            ws.append(a.astype(BF16))
            c = c + b[:, SB_BLOCK:2 * SB_BLOCK]
            n += 1
        out.append(ws)
        new_carries.append(c)
    return out, new_carries


def _sbp_kernel(bias_ref, q_ref, k_ref, v_ref, nw_ref, uo_ref, g_ref, o_ref,
                k16_ref, v2_ref, qm_ref, acc_ref, car_ref, *, l, d):
    qi = pl.program_id(1)
    blk = SB_BLOCK
    npair = d // LANES
    nhalf = SB_QROWS // blk
    first = lax.broadcasted_iota(jnp.int32, (1, LANES), 1) < HEAD_DIM

    @pl.when(qi == 0)
    def _():
        k16_ref[...] = k_ref[...].astype(BF16)
        nkb = v2_ref.shape[0]
        for p in range(npair):
            vv = v_ref[:, p * LANES:(p + 1) * LANES].reshape(nkb, blk, LANES)
            v2_ref[:, p, 0:blk, :] = jnp.where(first, vv, 0.0).astype(BF16)
            v2_ref[:, p, blk:2 * blk, :] = jnp.where(first, 0.0, vv).astype(BF16)

    for p in range(npair):
        for hf in range(nhalf):
            qs = q_ref[hf * blk:(hf + 1) * blk, p * LANES:(p + 1) * LANES] * (HEAD_DIM ** -0.5 * LOG2E)
            qm_ref[p, hf, 0:blk, :] = jnp.where(first, qs, 0.0).astype(BF16)
            qm_ref[p, hf, blk:2 * blk, :] = jnp.where(first, 0.0, qs).astype(BF16)

    def chain_id(p, hf, par):
        return (p * nhalf + hf) * 2 + par

    def stage(plan, carries):
        chains, masks, keys = [], [], []
        for p in range(npair):
            sl = slice(p * LANES, (p + 1) * LANES)
            for hf in range(nhalf):
                tiles = plan[(p, hf)]
                zs = [_dot_nt(qm_ref[p, hf], k16_ref[pl.ds(pl.multiple_of(kb * blk, blk), blk), sl]) for kb, _ in tiles]
                for par in range(2):
                    bias = bias_ref[l, 2 * p + par] * LOG2E
                    chains.append([z[par * blk:(par + 1) * blk] + bias for z in zs])
                    masks.append([m for _, m in tiles])
                    keys.append(chain_id(p, hf, par))
        ws, cs = _sb_stage(chains, masks, [carries[c] for c in keys], uo_ref)
        outs, n = {}, 0
        for p in range(npair):
            for hf in range(nhalf):
                o = None
                for i, (kb, _) in enumerate(plan[(p, hf)]):
                    t = _dot(jnp.concatenate([ws[n][i], ws[n + 1][i]], axis=1), v2_ref[kb, p])
                    o = t if o is None else o + t
                outs[(p, hf)] = o
                n += 2
        return outs, dict(zip(keys, cs))

    ti = lax.broadcasted_iota(jnp.int32, (blk, blk), 0)
    si = lax.broadcasted_iota(jnp.int32, (blk, blk), 1)
    causal = si < ti
    zero = jnp.zeros((blk, LANES), F32)
    nchain = npair * nhalf * 2

    d0 = qi * nhalf
    d1 = d0 + 1
    plan = {}
    for p in range(npair):
        plan[(p, 0)] = [(d0, causal)]
        plan[(p, 1)] = [(d1, causal), (d0, None)]
    outs, cs = stage(plan, {c: zero for c in range(nchain)})
    for (p, hf), o in outs.items():
        acc_ref[hf * blk:(hf + 1) * blk, p * LANES:(p + 1) * LANES] = o
    for c, v in cs.items():
        car_ref[c] = v

    def body(j, _):
        kb = d0 - 1 - j
        plan = {(p, hf): [(kb, None)] for p in range(npair) for hf in range(nhalf)}
        outs, cs = stage(plan, {c: car_ref[c] for c in range(nchain)})
        for (p, hf), o in outs.items():
            acc_ref[hf * blk:(hf + 1) * blk, p * LANES:(p + 1) * LANES] += o
        for c, v in cs.items():
            car_ref[c] = v
        return 0

    lax.fori_loop(0, nhalf * qi, body, 0)
    o_ref[...] = _head_rms(acc_ref[...], g_ref, nw_ref[...])


def _sb_prompt(q, k, v, bias, norm_w, l):
    bsz, t, d = q.shape
    blk = SB_BLOCK
    qr = SB_QROWS
    npair = d // LANES
    uo = jnp.asarray(_sb_consts(), BF16)
    gavg = jnp.asarray(_head_avg(d), BF16)
    full = lambda a: pl.BlockSpec(a.shape, lambda s, qi: (0,) * a.ndim)
    return pl.pallas_call(
        functools.partial(_sbp_kernel, l=l, d=d),
        out_shape=jax.ShapeDtypeStruct((bsz, t, d), F32),
        grid=(bsz, t // qr),
        in_specs=[
            pl.BlockSpec(memory_space=pltpu.SMEM),
            pl.BlockSpec((None, qr, d), lambda s, qi: (s, qi, 0)),
            pl.BlockSpec((None, t, d), lambda s, qi: (s, 0, 0)),
            pl.BlockSpec((None, t, d), lambda s, qi: (s, 0, 0)),
            pl.BlockSpec((None, 1, d), lambda s, qi: (l, 0, 0)),
            full(uo), full(gavg),
        ],
        out_specs=pl.BlockSpec((None, qr, d), lambda s, qi: (s, qi, 0)),
        scratch_shapes=[pltpu.VMEM((t, d), BF16), pltpu.VMEM((t // blk, npair, 2 * blk, LANES), BF16),
                        pltpu.VMEM((npair, qr // blk, 2 * blk, LANES), BF16), pltpu.VMEM((qr, d), F32),
                        pltpu.VMEM((npair * (qr // blk) * 2, blk, LANES), F32)],
        compiler_params=_cparams("parallel", "arbitrary"),
        name="sb_prompt",
    )(bias, q, k, v, norm_w.reshape(norm_w.shape[0], 1, d), uo, gavg)


def _sbs_kernel(pt_ref, bias_ref, q_ref, kn_ref, vn_ref, nw_ref, uo_ref, g_ref, ck_hbm, cv_hbm, o_ref,
                kbuf, vbuf, sem, qbd_ref, acc_ref, car_ref, *, l, d, nh, tq, nstep, nseq, npages):
    npg = PAGES_PER_STEP
    s = pl.program_id(0)
    j = pl.program_id(1)
    step = s * nstep + j
    slot = step % 2
    rows = nh * tq
    blk = SB_BLOCK

    def page_copies(seq, jj, slot_):
        out = []
        for i in range(npg):
            page = pt_ref[seq, npages - 1 - (jj * npg + i)]
            out.append(pltpu.make_async_copy(ck_hbm.at[l, page], kbuf.at[slot_, i], sem.at[0, slot_]))
            out.append(pltpu.make_async_copy(cv_hbm.at[l, page], vbuf.at[slot_, i], sem.at[1, slot_]))
        return out

    @pl.when(step == 0)
    def _():
        for c in page_copies(0, 0, 0):
            c.start()

    @pl.when(step + 1 < nseq * nstep)
    def _():
        wrap = j + 1 == nstep
        for c in page_copies(jnp.where(wrap, s + 1, s), jnp.where(wrap, 0, j + 1), 1 - slot):
            c.start()

    for c in page_copies(s, j, slot):
        c.wait()

    lane_d = lax.broadcasted_iota(jnp.int32, (rows, d), 1) // HEAD_DIM
    rgrp = lax.broadcasted_iota(jnp.int32, (rows, d), 0) // tq
    own = lane_d == rgrp

    rg1 = lax.broadcasted_iota(jnp.int32, (rows, blk), 0) // tq
    bias = jnp.zeros((rows, blk), F32)
    for h in range(nh):
        bias = jnp.where(rg1 == h, bias_ref[l, h] * LOG2E, bias)

    @pl.when(j == 0)
    def _():
        qs = q_ref[...] * (HEAD_DIM ** -0.5 * LOG2E)
        qbd = jnp.where(own, jnp.concatenate([qs] * nh, axis=0), 0.0).astype(BF16)
        qbd_ref[...] = qbd
        tpos = lax.broadcasted_iota(jnp.int32, (rows, blk), 0) % tq
        spos = lax.broadcasted_iota(jnp.int32, (rows, blk), 1)
        z = _dot_nt(qbd, kn_ref[...].astype(BF16)) + bias
        ws, cs = _sb_stage([[z]], [[spos < tpos]], [jnp.zeros((rows, blk), F32)], uo_ref)
        acc_ref[...] = _dot(ws[0][0], vn_ref[...].astype(BF16))
        car_ref[...] = cs[0]

    qbd = qbd_ref[...]
    zs = [_dot(qbd, kbuf[slot, i].reshape(d, blk).astype(BF16)) + bias for i in range(npg)]
    ws, cs = _sb_stage([zs], [[None] * npg], [car_ref[...]], uo_ref)
    acc = acc_ref[...]
    for i in range(npg):
        acc = acc + _dot_nt(ws[0][i], vbuf[slot, i].reshape(d, blk).astype(BF16))
    acc_ref[...] = acc
    car_ref[...] = cs[0]

    @pl.when(j == nstep - 1)
    def _():
        full = jnp.where(own, acc_ref[...], 0.0)
        o = full[0:tq]
        for h in range(1, nh):
            o = o + full[h * tq:(h + 1) * tq]
        o_ref[...] = _head_rms(o, g_ref, nw_ref[...])


def _sb_sample(q, k_new, v_new, cache_kt, cache_vt, page_table, bias, norm_w, l):
    bsz, tq, d = q.shape
    nh = d // HEAD_DIM
    npages = page_table.shape[1]
    npg = PAGES_PER_STEP
    nstep = npages // npg
    blk = SB_BLOCK
    uo = jnp.asarray(_sb_consts(), BF16)
    gavg = jnp.asarray(_head_avg(d), BF16)
    full = lambda a: pl.BlockSpec(a.shape, lambda s, j, pt: (0,) * a.ndim)
    grid_spec = pltpu.PrefetchScalarGridSpec(
        num_scalar_prefetch=1,
        grid=(bsz, nstep),
        in_specs=[
            pl.BlockSpec(memory_space=pltpu.SMEM),
            pl.BlockSpec((None, tq, d), lambda s, j, pt: (s, 0, 0)),
            pl.BlockSpec((None, blk, d), lambda s, j, pt: (s, 0, 0)),
            pl.BlockSpec((None, blk, d), lambda s, j, pt: (s, 0, 0)),
            pl.BlockSpec((None, 1, d), lambda s, j, pt: (l, 0, 0)),
            full(uo), full(gavg),
            pl.BlockSpec(memory_space=pl.ANY),
            pl.BlockSpec(memory_space=pl.ANY),
        ],
        out_specs=pl.BlockSpec((None, tq, d), lambda s, j, pt: (s, 0, 0)),
        scratch_shapes=[pltpu.VMEM((2, npg, nh, HEAD_DIM, blk), F32), pltpu.VMEM((2, npg, nh, HEAD_DIM, blk), F32),
                        pltpu.SemaphoreType.DMA((2, 2)),
                        pltpu.VMEM((nh * tq, d), BF16), pltpu.VMEM((nh * tq, d), F32),
                        pltpu.VMEM((nh * tq, blk), F32)],
    )
    return pl.pallas_call(
        functools.partial(_sbs_kernel, l=l, d=d, nh=nh, tq=tq, nstep=nstep, nseq=bsz, npages=npages),
        out_shape=jax.ShapeDtypeStruct((bsz, tq, d), F32),
        grid_spec=grid_spec,
        compiler_params=_cparams("arbitrary", "arbitrary"),
        name="sb_sample",
    )(page_table, bias, q, k_new, v_new, norm_w.reshape(norm_w.shape[0], 1, d), uo, gavg, cache_kt, cache_vt)


def _mixout_kernel(h_ref, c_ref, g_ref, s_ref, nw_ref, w_ref, o_ref, *, dc, dh):
    mix = _dot(c_ref[...].astype(BF16), w_ref[0:dc, :])
    mix = mix + _dot(g_ref[...].astype(BF16), w_ref[dc:dc + dh, :])
    mix = mix + _dot(s_ref[...].astype(BF16), w_ref[dc + dh:, :])
    o_ref[...] = h_ref[...] + _rms(mix, nw_ref[3:4, :])


def _mixout(h, conv_o, hgrn_o, sb_o, nw, w_out, l):
    m, d = h.shape
    tm = min(_row_tile(m), 512)
    dc, dh, ds = conv_o.shape[1], hgrn_o.shape[1], sb_o.shape[1]
    return pl.pallas_call(
        functools.partial(_mixout_kernel, dc=dc, dh=dh),
        out_shape=jax.ShapeDtypeStruct((m, d), F32),
        grid=(m // tm,),
        in_specs=[
            pl.BlockSpec((tm, d), lambda r: (r, 0)),
            pl.BlockSpec((tm, dc), lambda r: (r, 0)),
            pl.BlockSpec((tm, dh), lambda r: (r, 0)),
            pl.BlockSpec((tm, ds), lambda r: (r, 0)),
            pl.BlockSpec((None,) + nw.shape[1:], lambda r: (l, 0, 0)),
            pl.BlockSpec((None,) + w_out.shape[1:], lambda r: (l, 0, 0)),
        ],
        out_specs=pl.BlockSpec((tm, d), lambda r: (r, 0)),
        compiler_params=_cparams("parallel"),
        name="mixout",
    )(h, conv_o, hgrn_o, sb_o, nw, w_out)


def _ple_kernel(h_ref, p_ref, nw_ref, wu_ref, wg_ref, o_ref):
    h = h_ref[...]
    up = _dot(p_ref[...].astype(BF16), wu_ref[...])
    gate = _sigmoid(_dot(h.astype(BF16), wg_ref[...]))
    o_ref[...] = h + _rms(up * gate, nw_ref[6:7, :])


def _ple(h, p, nw, w_up, w_gate, l):
    m, d = h.shape
    dp = p.shape[-1]
    tm = min(_row_tile(m), 512)
    return pl.pallas_call(
        _ple_kernel,
        out_shape=jax.ShapeDtypeStruct((m, d), F32),
        grid=(m // tm,),
        in_specs=[
            pl.BlockSpec((tm, d), lambda r: (r, 0)),
            pl.BlockSpec((None, tm, dp), lambda r: (l, r, 0)),
            pl.BlockSpec((None,) + nw.shape[1:], lambda r: (l, 0, 0)),
            pl.BlockSpec((None,) + w_up.shape[1:], lambda r: (l, 0, 0)),
            pl.BlockSpec((None,) + w_gate.shape[1:], lambda r: (l, 0, 0)),
        ],
        out_specs=pl.BlockSpec((tm, d), lambda r: (r, 0)),
        compiler_params=_cparams("parallel"),
        name="ple",
    )(h, p, nw, w_up, w_gate)


def _layer(h, p_all, l, bsz, t, conv_prev, hgrn_prev, paged, prm):
    nw = prm["norm_w"]
    d_conv = prm["conv_w"].shape[-1]
    d_hgrn = prm["hgrn_norm_w"].shape[-1]
    d_sb = prm["sb_norm_w"].shape[-1]
    h = _ffn(h, nw, prm["ffn_w_gate"], prm["ffn_w_up"], prm["ffn_w_down"], l, 0, 0, 1)
    ab, hg, sq, sk, sv = _proj(h, nw, prm["w_in"], l, (2 * d_conv, 4 * d_hgrn, d_sb, d_sb, d_sb))

    conv_o, conv_state = _conv(ab.reshape(bsz, t, 2 * d_conv), conv_prev, prm["conv_w"], prm["conv_b"],
                               prm["conv_ln_w"], prm["conv_ln_b"], l)

    hg = hg.reshape(bsz, t, 4 * d_hgrn)
    tp = -(-t // HGRN_CHUNK) * HGRN_CHUNK
    if tp != t:
        hg = jnp.pad(hg, ((0, 0), (0, tp - t), (0, 0)))
    hgrn_o, hgrn_state = _hgrn(hg, hgrn_prev, prm["hgrn_lb_logits"], prm["hgrn_norm_w"], l, t)
    hgrn_o = hgrn_o[:, :t]

    q3, k3, v3 = (a.reshape(bsz, t, d_sb) for a in (sq, sk, sv))
    if paged is None:
        sb_o = _sb_prompt(q3, k3, v3, prm["sb_bias"], prm["sb_norm_w"], l)
    else:
        cache_kt, cache_vt, page_table = paged
        tq = -(-t // 8) * 8
        qpad = jnp.pad(q3, ((0, 0), (0, tq - t), (0, 0)))
        kpad = jnp.pad(k3, ((0, 0), (0, SB_BLOCK - t), (0, 0)))
        vpad = jnp.pad(v3, ((0, 0), (0, SB_BLOCK - t), (0, 0)))
        sb_o = _sb_sample(qpad, kpad, vpad, cache_kt, cache_vt, page_table, prm["sb_bias"],
                          prm["sb_norm_w"], l)[:, :t]

    h = _mixout(h, conv_o.reshape(bsz * t, d_conv), hgrn_o.reshape(bsz * t, d_hgrn),
                sb_o.reshape(bsz * t, d_sb), nw, prm["w_out"], l)
    h = _ffn(h, nw, prm["ffn_w_gate"], prm["ffn_w_up"], prm["ffn_w_down"], l, 1, 4, 5)
    h = _ple(h, p_all, nw, prm["ple_w_up"], prm["ple_w_gate"], l)
    return h, (sk, sv, conv_state, hgrn_state)


def kernel(x_prompt, x_sample, cache_k, cache_v, state_conv, state_hgrn, page_table, p_prompt, p_sample, norm_w, ffn_w_gate, ffn_w_up, ffn_w_down, w_in, conv_w, conv_b, conv_ln_w, conv_ln_b, hgrn_lb_logits, hgrn_norm_w, sb_bias, sb_norm_w, w_out, ple_w_up, ple_w_gate):
    depth = w_in.shape[0]
    bp, tpr, dm = x_prompt.shape
    bs, ts, _ = x_sample.shape
    n_sb = sb_bias.shape[1]
    n_hg = state_hgrn.shape[2]
    d_conv = conv_w.shape[-1]
    prm = dict(norm_w=norm_w, ffn_w_gate=ffn_w_gate.astype(BF16), ffn_w_up=ffn_w_up.astype(BF16),
               ffn_w_down=ffn_w_down.astype(BF16), w_in=w_in.astype(BF16), conv_w=conv_w, conv_b=conv_b,
               conv_ln_w=conv_ln_w, conv_ln_b=conv_ln_b, hgrn_lb_logits=hgrn_lb_logits,
               hgrn_norm_w=hgrn_norm_w, sb_bias=sb_bias, sb_norm_w=sb_norm_w, w_out=w_out.astype(BF16),
               ple_w_up=ple_w_up.astype(BF16), ple_w_gate=ple_w_gate.astype(BF16))

    h_p = x_prompt.reshape(bp * tpr, dm)
    h_s = x_sample.reshape(bs * ts, dm)
    pp = p_prompt.reshape(depth, bp * tpr, -1)
    ps = p_sample.reshape(depth, bs * ts, -1)
    conv0 = jnp.zeros((bp, CONV_WIDTH - 1, d_conv), F32)
    hgrn0 = jnp.zeros((bp, n_hg // 2, LANES, LANES), F32)
    cache_kt = jnp.transpose(cache_k, (0, 1, 3, 4, 2))
    cache_vt = jnp.transpose(cache_v, (0, 1, 3, 4, 2))
    outs = [[] for _ in range(8)]
    for l in range(depth):
        h_p, st_p = _layer(h_p, pp, l, bp, tpr, conv0, hgrn0, None, prm)
        h_s, st_s = _layer(h_s, ps, l, bs, ts, state_conv[l], _pack_state(state_hgrn[l]),
                           (cache_kt, cache_vt, page_table), prm)
        for i, a in enumerate(st_p + st_s):
            outs[i].append(a)

    def heads(xs, b, t, n):
        return jnp.stack(xs).reshape(depth, b, t, n, HEAD_DIM)

    return (h_p.reshape(bp, tpr, dm), h_s.reshape(bs, ts, dm),
            heads(outs[0], bp, tpr, n_sb), heads(outs[1], bp, tpr, n_sb),
            jnp.stack(outs[2]), jnp.stack([_unpack_state(s) for s in outs[3]]),
            heads(outs[4], bs, ts, n_sb), heads(outs[5], bs, ts, n_sb),
            jnp.stack(outs[6]), jnp.stack([_unpack_state(s) for s in outs[7]]))
```

```python
import functools

import numpy as np
import jax
import jax.numpy as jnp
from jax import lax
from jax.experimental import pallas as pl
from jax.experimental.pallas import tpu as pltpu

F32 = jnp.float32
BF16 = jnp.bfloat16

EPS = 1e-6
HEAD_DIM = 64
LANES = 128
CONV_WIDTH = 31
CONV_TILE = 64
CONV_UNROLL = 4
CONV_PAD = 32
FFN_TF = 256
FFN_TF_WIDE_BELOW = 1024
HGRN_CHUNK = 64
HGRN_MIN_CHUNK = 16
SB_BLOCK = 128
SB_QROWS = 256
PAGES_PER_STEP = 8
PAGE_SLOTS = 3
VMEM_LIMIT = 48 * 1024 * 1024


def _cparams(*sem):
    return pltpu.CompilerParams(dimension_semantics=sem, vmem_limit_bytes=VMEM_LIMIT)


def _rms(x, w):
    ms = jnp.mean(x * x, axis=-1, keepdims=True)
    return x * lax.rsqrt(ms + EPS) * w


def _sigmoid(x):
    return 1.0 / (1.0 + jnp.exp(-x))


def _softplus(x):
    return jnp.maximum(x, 0.0) + jnp.log1p(jnp.exp(-jnp.abs(x)))


def _dot(a, b):
    return jnp.dot(a, b, preferred_element_type=F32)


def _dot_nt(a, b):
    return lax.dot_general(a, b, (((1,), (1,)), ((), ())), preferred_element_type=F32)


def _dot_tn(a, b):
    return lax.dot_general(a, b, (((0,), (0,)), ((), ())), preferred_element_type=F32)


def _split(x):
    hi = x.astype(BF16)
    lo = (x - hi.astype(F32)).astype(BF16)
    return hi, lo


def _dot_split(x, w):
    hi, lo = _split(x)
    return _dot(hi, w) + _dot(lo, w)


def _head_rms(o, g_ref, w):
    ms = _dot_split(o * o, g_ref[...])
    return o * lax.rsqrt(ms + EPS) * w


def _row_tile(m):
    for t in (1024, 512, 256, 128, 64, 32, 16, 8):
        if m % t == 0:
            return t
    raise ValueError(f"token count {m} must be a multiple of 8")


def _ffn_kernel(h_ref, nw_ref, wg_ref, wu_ref, wd_ref, o_ref, xn_ref, acc_ref, *, pre, post, nf):
    f = pl.program_id(1)

    @pl.when(f == 0)
    def _():
        xn_ref[...] = _rms(h_ref[...], nw_ref[pre:pre + 1, :]).astype(BF16)
        acc_ref[...] = jnp.zeros_like(acc_ref)

    xn = xn_ref[...]
    g = _dot(xn, wg_ref[...])
    u = _dot(xn, wu_ref[...])
    a = (g * _sigmoid(g) * u).astype(BF16)
    acc_ref[...] += _dot(a, wd_ref[...])

    @pl.when(f == nf - 1)
    def _():
        o_ref[...] = h_ref[...] + 0.5 * _rms(acc_ref[...], nw_ref[post:post + 1, :])


def _ffn(h, nw, wg, wu, wd, l, i, pre, post):
    m, d = h.shape
    dff = wg.shape[-1]
    tm = _row_tile(m)
    tf = FFN_TF if m >= FFN_TF_WIDE_BELOW else dff // 2
    nf = dff // tf
    return pl.pallas_call(
        functools.partial(_ffn_kernel, pre=pre, post=post, nf=nf),
        out_shape=jax.ShapeDtypeStruct((m, d), F32),
        grid=(m // tm, nf),
        in_specs=[
            pl.BlockSpec((tm, d), lambda r, f: (r, 0)),
            pl.BlockSpec((None,) + nw.shape[1:], lambda r, f: (l, 0, 0)),
            pl.BlockSpec((None, None, d, tf), lambda r, f: (l, i, 0, f)),
            pl.BlockSpec((None, None, d, tf), lambda r, f: (l, i, 0, f)),
            pl.BlockSpec((None, None, tf, d), lambda r, f: (l, i, f, 0)),
        ],
        out_specs=pl.BlockSpec((tm, d), lambda r, f: (r, 0)),
        scratch_shapes=[pltpu.VMEM((tm, d), BF16), pltpu.VMEM((tm, d), F32)],
        compiler_params=_cparams("parallel", "arbitrary"),
        name="ffn",
    )(h, nw, wg, wu, wd)


def _proj_kernel(h_ref, nw_ref, w_ref, conv_ref, hg_ref, q_ref, k_ref, v_ref, *, bounds):
    xn = _rms(h_ref[...], nw_ref[2:3, :]).astype(BF16)
    for ref, (lo, hi) in zip((conv_ref, hg_ref, q_ref, k_ref, v_ref), bounds):
        ref[...] = _dot(xn, w_ref[:, lo:hi])


def _proj(h, nw, w_in, l, widths):
    m, d = h.shape
    tm = min(_row_tile(m), 512)
    bounds, off = [], 0
    for w in widths:
        bounds.append((off, off + w))
        off += w
    return pl.pallas_call(
        functools.partial(_proj_kernel, bounds=tuple(bounds)),
        out_shape=[jax.ShapeDtypeStruct((m, w), F32) for w in widths],
        grid=(m // tm,),
        in_specs=[
            pl.BlockSpec((tm, d), lambda r: (r, 0)),
            pl.BlockSpec((None,) + nw.shape[1:], lambda r: (l, 0, 0)),
            pl.BlockSpec((None,) + w_in.shape[1:], lambda r: (l, 0, 0)),
        ],
        out_specs=[pl.BlockSpec((tm, w), lambda r: (r, 0)) for w in widths],
        compiler_params=_cparams("parallel"),
        name="proj",
    )(h, nw, w_in)


def _conv_kernel(ab_ref, prev_ref, w_ref, b_ref, lnw_ref, lnb_ref, y_ref, st_ref, up_ref, *, t, rt, dc):
    npad = CONV_WIDTH - 1
    first = CONV_PAD - npad
    a = ab_ref[:, 0:dc]
    b = ab_ref[:, dc:2 * dc]
    up_ref[0:first, :] = jnp.zeros((first, dc), F32)
    up_ref[first:CONV_PAD, :] = prev_ref[...]
    up_ref[CONV_PAD:CONV_PAD + t, :] = a * _sigmoid(b)
    st_ref[...] = up_ref[first + t:first + t + npad, :]

    groups = [[] for _ in range(8)]
    for j in range(CONV_WIDTH):
        blk8, rem = divmod(j + first, 8)
        groups[rem].append((blk8, j))

    def tile(base):
        y = None
        for rem, taps in enumerate(groups):
            if not taps:
                continue
            n = rt + 8 if rem else rt
            part = None
            for blk8, j in taps:
                term = w_ref[j:j + 1, :] * up_ref[pl.ds(base + 8 * blk8, n), :]
                part = term if part is None else part + term
            if rem:
                part = part[rem:rem + rt]
            y = part if y is None else y + part
        y = y + b_ref[...]
        mu = jnp.mean(y, axis=-1, keepdims=True)
        yc = y - mu
        var = jnp.mean(yc * yc, axis=-1, keepdims=True)
        yn = yc * lax.rsqrt(var + EPS) * lnw_ref[...] + lnb_ref[...]
        y_ref[pl.ds(base, rt), :] = yn * _sigmoid(yn)

    if t == rt:
        tile(0)
    else:
        def body(i, _):
            for u in range(CONV_UNROLL):
                tile(pl.multiple_of((i * CONV_UNROLL + u) * rt, rt))
            return 0
        lax.fori_loop(0, t // (rt * CONV_UNROLL), body, 0)


def _conv(ab, prev, w, b, lnw, lnb, l):
    bsz, t, dc2 = ab.shape
    dc = dc2 // 2
    rt = CONV_TILE if t % (CONV_TILE * CONV_UNROLL) == 0 else t
    vec = lambda a: pl.BlockSpec((None, 1, dc), lambda s: (l, 0, 0))
    return pl.pallas_call(
        functools.partial(_conv_kernel, t=t, rt=rt, dc=dc),
        out_shape=[jax.ShapeDtypeStruct((bsz, t, dc), F32),
                   jax.ShapeDtypeStruct((bsz, CONV_WIDTH - 1, dc), F32)],
        grid=(bsz,),
        in_specs=[
            pl.BlockSpec((None, t, dc2), lambda s: (s, 0, 0)),
            pl.BlockSpec((None, CONV_WIDTH - 1, dc), lambda s: (s, 0, 0)),
            pl.BlockSpec((None, CONV_WIDTH, dc), lambda s: (l, 0, 0)),
            vec(b), vec(lnw), vec(lnb),
        ],
        out_specs=[pl.BlockSpec((None, t, dc), lambda s: (s, 0, 0)),
                   pl.BlockSpec((None, CONV_WIDTH - 1, dc), lambda s: (s, 0, 0))],
        scratch_shapes=[pltpu.VMEM((CONV_PAD + t, dc), F32)],
        compiler_params=_cparams("parallel"),
        name="conv",
    )(ab, prev, w, b.reshape(b.shape[0], 1, dc), lnw.reshape(lnw.shape[0], 1, dc),
      lnb.reshape(lnb.shape[0], 1, dc))


def _hgrn_consts(c):
    levels = []
    m = c // 2
    while m >= 1:
        levels.append(m)
        m //= 2
    idx = np.arange(c)
    mats = [(idx[None, :] <= idx[:, None]).astype(np.float32)]
    masks = []
    for m in levels:
        blk, half = idx // (2 * m), (idx // m) % 2
        same = blk[:, None] == blk[None, :]
        upper = same & (half[:, None] == 1) & (half[None, :] == 1) & (idx[None, :] <= idx[:, None])
        lower = same & (half[:, None] == 0) & (half[None, :] == 0) & (idx[None, :] > idx[:, None])
        mats += [upper.astype(np.float32), lower.astype(np.float32)]
        masks.append((same & (half[:, None] == 1) & (half[None, :] == 0)).astype(np.float32))
    masks.append(np.eye(c, dtype=np.float32))
    masks = np.stack(masks)
    masks = np.concatenate([masks, masks], axis=1)
    return len(levels), np.concatenate(mats, axis=0), masks


def _head_avg(d):
    i = np.arange(d) // HEAD_DIM
    return (i[:, None] == i[None, :]).astype(np.float32) / HEAD_DIM


def _hgrn_kernel(hg_ref, s0_ref, lbl_ref, nw_ref, mall_ref, mask_ref, g_ref, o_ref, sT_ref, st_ref,
                 *, l, c, cps, d, nlev, nsteps, t_valid):
    step = pl.program_id(1)
    npair = d // LANES

    @pl.when(step == 0)
    def _():
        st_ref[...] = s0_ref[...]

    lg = lbl_ref[...]
    e = jnp.exp(lg - jnp.max(lg, axis=0, keepdims=True))
    sm = e / jnp.sum(e, axis=0, keepdims=True)
    row = lax.broadcasted_iota(jnp.int32, sm.shape, 0)
    lb = jnp.sum(jnp.where((row >= 1) & (row <= l), sm, 0.0), axis=0, keepdims=True)
    log_lb = jnp.log(lb)
    log_1mlb = jnp.log1p(-lb)

    ones16 = jnp.ones((2 * c, LANES), BF16)
    lane = lax.broadcasted_iota(jnp.int32, (c, LANES), 1)
    first = lane < HEAD_DIM
    ri = lax.broadcasted_iota(jnp.int32, (LANES, LANES), 0) // HEAD_DIM
    cj = lax.broadcasted_iota(jnp.int32, (LANES, LANES), 1) // HEAD_DIM
    blockdiag = ri == cj
    mall = mall_ref[...]

    prep = []
    for cc in range(cps):
        rows = slice(cc * c, (cc + 1) * c)
        q = hg_ref[rows, 0:d]
        zf = hg_ref[rows, d:2 * d]

        bb = log_1mlb - _softplus(-zf)
        log_f = jnp.maximum(log_lb, bb) + jnp.log1p(jnp.exp(-jnp.abs(log_lb - bb)))
        kk = (1.0 - lb) * _sigmoid(-zf)
        if t_valid < nsteps * cps * c:
            trow = lax.broadcasted_iota(jnp.int32, (c, d), 0) + (step * cps + cc) * c
            live = trow < t_valid
            log_f = jnp.where(live, log_f, 0.0)
            kk = jnp.where(live, kk, 0.0)
        lf2 = jnp.concatenate(_split(log_f), axis=0)
        prep.append((q, kk, lf2))

    e_alls = [_dot(mall, lf2) for _, _, lf2 in prep]

    amats = {}
    for i in range(nlev + 1):
        for cc in range(cps):
            q, kk, _ = prep[cc]
            e_all = e_alls[cc]
            for p in range(npair):
                sl = slice(p * LANES, (p + 1) * LANES)
                if i < nlev:
                    ql = q[:, sl] * jnp.exp(e_all[(1 + 2 * i) * c:(2 + 2 * i) * c, sl])
                    kl = kk[:, sl] * jnp.exp(e_all[(2 + 2 * i) * c:(3 + 2 * i) * c, sl])
                else:
                    ql, kl = q[:, sl], kk[:, sl]
                qst = jnp.concatenate([jnp.where(first, ql, 0.0), jnp.where(first, 0.0, ql)], axis=0)
                term = mask_ref[i] * _dot_nt(qst.astype(BF16), kl.astype(BF16))
                amats[(cc, p)] = term if i == 0 else amats[(cc, p)] + term

    for cc in range(cps):
        rows = slice(cc * c, (cc + 1) * c)
        q, kk, lf2 = prep[cc]
        bcum = e_alls[cc][0:c]
        b_last = bcum[c - 1:c, :]
        qe = (q * jnp.exp(bcum)).astype(BF16)
        kd = (kk * jnp.exp(b_last - bcum)).astype(BF16)
        v16 = hg_ref[rows, 2 * d:3 * d].astype(BF16)
        gate = hg_ref[rows, 3 * d:4 * d]
        outs = []
        for p in range(npair):
            sl = slice(p * LANES, (p + 1) * LANES)
            a16 = amats[(cc, p)].astype(BF16)
            vp = v16[:, sl]
            zero = jnp.zeros_like(vp)
            state = st_ref[p]
            o = _dot(qe[:, sl], state.astype(BF16))
            o = o + _dot(a16[0:c], jnp.where(first, vp, zero)) + _dot(a16[c:2 * c], jnp.where(first, zero, vp))
            outs.append(o)
            elog = _dot_tn(lf2[:, sl], ones16)
            upd = _dot_tn(kd[:, sl], vp)
            st_ref[p] = jnp.exp(elog) * state + jnp.where(blockdiag, upd, 0.0)
        o = jnp.concatenate(outs, axis=1)
        o_ref[rows, :] = _head_rms(o, g_ref, nw_ref[...]) * (gate * _sigmoid(gate))

    @pl.when(step == nsteps - 1)
    def _():
        sT_ref[...] = st_ref[...]


def _hgrn(hg, s0p, lb_logits, norm_w, l, t_valid, c):
    bsz, t, d4 = hg.shape
    d = d4 // 4
    cps = 2 if t % (2 * c) == 0 else 1
    nsteps = t // (c * cps)
    nlev, mall, masks = _hgrn_consts(c)
    npair = d // LANES
    full = lambda a: pl.BlockSpec(a.shape, lambda s, ci: (0,) * a.ndim)
    mall = jnp.asarray(np.concatenate([mall, mall], axis=1), BF16)
    masks = jnp.asarray(masks, F32)
    gavg = jnp.asarray(_head_avg(d), BF16)
    return pl.pallas_call(
        functools.partial(_hgrn_kernel, l=l, c=c, cps=cps, d=d, nlev=nlev, nsteps=nsteps, t_valid=t_valid),
        out_shape=[jax.ShapeDtypeStruct((bsz, t, d), F32),
                   jax.ShapeDtypeStruct((bsz, npair, LANES, LANES), F32)],
        grid=(bsz, nsteps),
        in_specs=[
            pl.BlockSpec((None, c * cps, d4), lambda s, ci: (s, ci, 0)),
            pl.BlockSpec((None, npair, LANES, LANES), lambda s, ci: (s, 0, 0, 0)),
            full(lb_logits),
            pl.BlockSpec((None, 1, d), lambda s, ci: (l, 0, 0)),
            full(mall), full(masks), full(gavg),
        ],
        out_specs=[pl.BlockSpec((None, c * cps, d), lambda s, ci: (s, ci, 0)),
                   pl.BlockSpec((None, npair, LANES, LANES), lambda s, ci: (s, 0, 0, 0))],
        scratch_shapes=[pltpu.VMEM((npair, LANES, LANES), F32)],
        compiler_params=_cparams("parallel", "arbitrary"),
        name="hgrn",
    )(hg, s0p, lb_logits, norm_w.reshape(norm_w.shape[0], 1, d), mall, masks, gavg)


def _pack_state(s):
    bsz, h = s.shape[:2]
    s = s.reshape(bsz, h // 2, 2, HEAD_DIM, HEAD_DIM)
    z = jnp.zeros_like(s[:, :, 0])
    top = jnp.concatenate([s[:, :, 0], z], axis=-1)
    bot = jnp.concatenate([z, s[:, :, 1]], axis=-1)
    return jnp.concatenate([top, bot], axis=-2)


def _unpack_state(sp):
    a = sp[:, :, :HEAD_DIM, :HEAD_DIM]
    b = sp[:, :, HEAD_DIM:, HEAD_DIM:]
    bsz, npair = sp.shape[:2]
    return jnp.stack([a, b], axis=2).reshape(bsz, 2 * npair, HEAD_DIM, HEAD_DIM)


LOG2E = 1.4426950408889634


def _sb_consts():
    j = np.arange(SB_BLOCK)
    after = (j[:, None] > j[None, :]).astype(np.float32)
    uo = np.concatenate([after, np.ones_like(after)], axis=1)
    return np.concatenate([uo, uo], axis=0)


def _softplus2(z):
    return jnp.maximum(z, 0.0) + jnp.log2(1.0 + jnp.exp2(-jnp.abs(z)))


def _sb_stage(chains, masks, carries, uo_ref):
    r = chains[0][0].shape[0]
    t1s, lhs = [], []
    for zs, ms in zip(chains, masks):
        for z, m in zip(zs, ms):
            sp = _softplus2(z)
            t1s.append(z - sp)
            if m is not None:
                sp = jnp.where(m, sp, 0.0)
            hi, lo = _split(sp)
            lhs.append(jnp.concatenate([hi, lo], axis=1))
    both = _dot(lhs[0] if len(lhs) == 1 else jnp.concatenate(lhs, axis=0), uo_ref[...])
    out, new_carries, n = [], [], 0
    for zs, ms, c in zip(chains, masks, carries):
        ws = []
        for m in ms:
            b = both[n * r:(n + 1) * r]
            a = jnp.exp2(t1s[n] - (c + b[:, 0:SB_BLOCK]))
            if m is not None:
                a = jnp.where(m, a, 0.0)
            ws.append(a.astype(BF16))
            c = c + b[:, SB_BLOCK:2 * SB_BLOCK]
            n += 1
        out.append(ws)
        new_carries.append(c)
    return out, new_carries


def _sbp_kernel(bias_ref, q_ref, k_ref, v_ref, nw_ref, uo_ref, g_ref, o_ref,
                k16_ref, v2_ref, qm_ref, acc_ref, car_ref, *, l, d):
    qi = pl.program_id(1)
    blk = SB_BLOCK
    npair = d // LANES
    nhalf = SB_QROWS // blk
    first = lax.broadcasted_iota(jnp.int32, (1, LANES), 1) < HEAD_DIM

    @pl.when(qi == 0)
    def _():
        k16_ref[...] = k_ref[...].astype(BF16)
        nkb = v2_ref.shape[0]
        for p in range(npair):
            vv = v_ref[:, p * LANES:(p + 1) * LANES].reshape(nkb, blk, LANES)
            v2_ref[:, p, 0:blk, :] = jnp.where(first, vv, 0.0).astype(BF16)
            v2_ref[:, p, blk:2 * blk, :] = jnp.where(first, 0.0, vv).astype(BF16)

    for p in range(npair):
        for hf in range(nhalf):
            qs = q_ref[hf * blk:(hf + 1) * blk, p * LANES:(p + 1) * LANES] * (HEAD_DIM ** -0.5 * LOG2E)
            qm_ref[p, hf, 0:blk, :] = jnp.where(first, qs, 0.0).astype(BF16)
            qm_ref[p, hf, blk:2 * blk, :] = jnp.where(first, 0.0, qs).astype(BF16)

    def chain_id(p, hf, par):
        return (p * nhalf + hf) * 2 + par

    def stage(plan, carries):
        chains, masks, keys = [], [], []
        for p in range(npair):
            sl = slice(p * LANES, (p + 1) * LANES)
            for hf in range(nhalf):
                tiles = plan[(p, hf)]
                zs = [_dot_nt(qm_ref[p, hf], k16_ref[pl.ds(pl.multiple_of(kb * blk, blk), blk), sl]) for kb, _ in tiles]
                for par in range(2):
                    bias = bias_ref[l, 2 * p + par] * LOG2E
                    chains.append([z[par * blk:(par + 1) * blk] + bias for z in zs])
                    masks.append([m for _, m in tiles])
                    keys.append(chain_id(p, hf, par))
        ws, cs = _sb_stage(chains, masks, [carries[c] for c in keys], uo_ref)
        outs, n = {}, 0
        for p in range(npair):
            for hf in range(nhalf):
                o = None
                for i, (kb, _) in enumerate(plan[(p, hf)]):
                    t = _dot(jnp.concatenate([ws[n][i], ws[n + 1][i]], axis=1), v2_ref[kb, p])
                    o = t if o is None else o + t
                outs[(p, hf)] = o
                n += 2
        return outs, dict(zip(keys, cs))

    ti = lax.broadcasted_iota(jnp.int32, (blk, blk), 0)
    si = lax.broadcasted_iota(jnp.int32, (blk, blk), 1)
    causal = si < ti
    zero = jnp.zeros((blk, LANES), F32)
    nchain = npair * nhalf * 2

    d0 = qi * nhalf
    d1 = d0 + 1
    plan = {}
    for p in range(npair):
        plan[(p, 0)] = [(d0, causal)]
        plan[(p, 1)] = [(d1, causal), (d0, None)]
    outs, cs = stage(plan, {c: zero for c in range(nchain)})
    for (p, hf), o in outs.items():
        acc_ref[hf * blk:(hf + 1) * blk, p * LANES:(p + 1) * LANES] = o
    for c, v in cs.items():
        car_ref[c] = v

    def body(j, _):
        kb = d0 - 1 - j
        plan = {(p, hf): [(kb, None)] for p in range(npair) for hf in range(nhalf)}
        outs, cs = stage(plan, {c: car_ref[c] for c in range(nchain)})
        for (p, hf), o in outs.items():
            acc_ref[hf * blk:(hf + 1) * blk, p * LANES:(p + 1) * LANES] += o
        for c, v in cs.items():
            car_ref[c] = v
        return 0

    lax.fori_loop(0, nhalf * qi, body, 0)
    o_ref[...] = _head_rms(acc_ref[...], g_ref, nw_ref[...])


def _sb_prompt(q, k, v, bias, norm_w, l):
    bsz, t, d = q.shape
    blk = SB_BLOCK
    qr = SB_QROWS
    npair = d // LANES
    uo = jnp.asarray(_sb_consts(), BF16)
    gavg = jnp.asarray(_head_avg(d), BF16)
    full = lambda a: pl.BlockSpec(a.shape, lambda s, qi: (0,) * a.ndim)
    return pl.pallas_call(
        functools.partial(_sbp_kernel, l=l, d=d),
        out_shape=jax.ShapeDtypeStruct((bsz, t, d), F32),
        grid=(bsz, t // qr),
        in_specs=[
            pl.BlockSpec(memory_space=pltpu.SMEM),
            pl.BlockSpec((None, qr, d), lambda s, qi: (s, qi, 0)),
            pl.BlockSpec((None, t, d), lambda s, qi: (s, 0, 0)),
            pl.BlockSpec((None, t, d), lambda s, qi: (s, 0, 0)),
            pl.BlockSpec((None, 1, d), lambda s, qi: (l, 0, 0)),
            full(uo), full(gavg),
        ],
        out_specs=pl.BlockSpec((None, qr, d), lambda s, qi: (s, qi, 0)),
        scratch_shapes=[pltpu.VMEM((t, d), BF16), pltpu.VMEM((t // blk, npair, 2 * blk, LANES), BF16),
                        pltpu.VMEM((npair, qr // blk, 2 * blk, LANES), BF16), pltpu.VMEM((qr, d), F32),
                        pltpu.VMEM((npair * (qr // blk) * 2, blk, LANES), F32)],
        compiler_params=_cparams("parallel", "arbitrary"),
        name="sb_prompt",
    )(bias, q, k, v, norm_w.reshape(norm_w.shape[0], 1, d), uo, gavg)


def _sbs_kernel(pt_ref, bias_ref, q_ref, kn_ref, vn_ref, nw_ref, uo_ref, g_ref, ck_hbm, cv_hbm, o_ref,
                kbuf, vbuf, sem, qbd_ref, acc_ref, car_ref, *, l, d, nh, tq, nstep, nseq, npages):
    npg = PAGES_PER_STEP
    s = pl.program_id(0)
    j = pl.program_id(1)
    step = s * nstep + j
    slot = step % PAGE_SLOTS
    rows = nh * tq
    blk = SB_BLOCK

    def page_copies(linear):
        seq, jj, slot_ = linear // nstep, linear % nstep, linear % PAGE_SLOTS
        out = []
        for i in range(npg):
            page = pt_ref[seq, npages - 1 - (jj * npg + i)]
            out.append((pltpu.make_async_copy(ck_hbm.at[l, page], kbuf.at[slot_, i], sem.at[0, slot_]),
                        pltpu.make_async_copy(cv_hbm.at[l, page], vbuf.at[slot_, i], sem.at[1, slot_])))
        return out

    def start(linear):
        for ck, cv in page_copies(linear):
            ck.start(priority=0)
            cv.start(priority=1)

    @pl.when(step == 0)
    def _():
        for ahead in range(PAGE_SLOTS - 1):
            start(ahead)

    @pl.when(step + (PAGE_SLOTS - 1) < nseq * nstep)
    def _():
        start(step + (PAGE_SLOTS - 1))

    for ck, cv in page_copies(step):
        ck.wait()
        cv.wait()

    lane_d = lax.broadcasted_iota(jnp.int32, (rows, d), 1) // HEAD_DIM
    rgrp = lax.broadcasted_iota(jnp.int32, (rows, d), 0) // tq
    own = lane_d == rgrp

    rg1 = lax.broadcasted_iota(jnp.int32, (rows, blk), 0) // tq
    bias = jnp.zeros((rows, blk), F32)
    for h in range(nh):
        bias = jnp.where(rg1 == h, bias_ref[l, h] * LOG2E, bias)

    @pl.when(j == 0)
    def _():
        qs = q_ref[...] * (HEAD_DIM ** -0.5 * LOG2E)
        qbd = jnp.where(own, jnp.concatenate([qs] * nh, axis=0), 0.0).astype(BF16)
        qbd_ref[...] = qbd
        tpos = lax.broadcasted_iota(jnp.int32, (rows, blk), 0) % tq
        spos = lax.broadcasted_iota(jnp.int32, (rows, blk), 1)
        z = _dot_nt(qbd, kn_ref[...].astype(BF16)) + bias
        ws, cs = _sb_stage([[z]], [[spos < tpos]], [jnp.zeros((rows, blk), F32)], uo_ref)
        acc_ref[...] = _dot(ws[0][0], vn_ref[...].astype(BF16))
        car_ref[...] = cs[0]

    qbd = qbd_ref[...]
    zs = [_dot(qbd, kbuf[slot, i].reshape(d, blk).astype(BF16)) + bias for i in range(npg)]
    ws, cs = _sb_stage([zs], [[None] * npg], [car_ref[...]], uo_ref)
    acc = acc_ref[...]
    for i in range(npg):
        acc = acc + _dot_nt(ws[0][i], vbuf[slot, i].reshape(d, blk).astype(BF16))
    acc_ref[...] = acc
    car_ref[...] = cs[0]

    @pl.when(j == nstep - 1)
    def _():
        full = jnp.where(own, acc_ref[...], 0.0)
        o = full[0:tq]
        for h in range(1, nh):
            o = o + full[h * tq:(h + 1) * tq]
        o_ref[...] = _head_rms(o, g_ref, nw_ref[...])


def _sb_sample(q, k_new, v_new, cache_kt, cache_vt, page_table, bias, norm_w, l):
    bsz, tq, d = q.shape
    nh = d // HEAD_DIM
    npages = page_table.shape[1]
    npg = PAGES_PER_STEP
    nstep = npages // npg
    blk = SB_BLOCK
    uo = jnp.asarray(_sb_consts(), BF16)
    gavg = jnp.asarray(_head_avg(d), BF16)
    full = lambda a: pl.BlockSpec(a.shape, lambda s, j, pt: (0,) * a.ndim)
    grid_spec = pltpu.PrefetchScalarGridSpec(
        num_scalar_prefetch=1,
        grid=(bsz, nstep),
        in_specs=[
            pl.BlockSpec(memory_space=pltpu.SMEM),
            pl.BlockSpec((None, tq, d), lambda s, j, pt: (s, 0, 0)),
            pl.BlockSpec((None, blk, d), lambda s, j, pt: (s, 0, 0)),
            pl.BlockSpec((None, blk, d), lambda s, j, pt: (s, 0, 0)),
            pl.BlockSpec((None, 1, d), lambda s, j, pt: (l, 0, 0)),
            full(uo), full(gavg),
            pl.BlockSpec(memory_space=pl.ANY),
            pl.BlockSpec(memory_space=pl.ANY),
        ],
        out_specs=pl.BlockSpec((None, tq, d), lambda s, j, pt: (s, 0, 0)),
        scratch_shapes=[pltpu.VMEM((PAGE_SLOTS, npg, nh, HEAD_DIM, blk), F32),
                        pltpu.VMEM((PAGE_SLOTS, npg, nh, HEAD_DIM, blk), F32),
                        pltpu.SemaphoreType.DMA((2, PAGE_SLOTS)),
                        pltpu.VMEM((nh * tq, d), BF16), pltpu.VMEM((nh * tq, d), F32),
                        pltpu.VMEM((nh * tq, blk), F32)],
    )
    return pl.pallas_call(
        functools.partial(_sbs_kernel, l=l, d=d, nh=nh, tq=tq, nstep=nstep, nseq=bsz, npages=npages),
        out_shape=jax.ShapeDtypeStruct((bsz, tq, d), F32),
        grid_spec=grid_spec,
        compiler_params=_cparams("arbitrary", "arbitrary"),
        name="sb_sample",
    )(page_table, bias, q, k_new, v_new, norm_w.reshape(norm_w.shape[0], 1, d), uo, gavg, cache_kt, cache_vt)


def _mixout_kernel(h_ref, c_ref, g_ref, s_ref, nw_ref, w_ref, o_ref, *, dc, dh):
    mix = _dot(c_ref[...].astype(BF16), w_ref[0:dc, :])
    mix = mix + _dot(g_ref[...].astype(BF16), w_ref[dc:dc + dh, :])
    mix = mix + _dot(s_ref[...].astype(BF16), w_ref[dc + dh:, :])
    o_ref[...] = h_ref[...] + _rms(mix, nw_ref[3:4, :])


def _mixout(h, conv_o, hgrn_o, sb_o, nw, w_out, l):
    m, d = h.shape
    tm = min(_row_tile(m), 512)
    dc, dh, ds = conv_o.shape[1], hgrn_o.shape[1], sb_o.shape[1]
    return pl.pallas_call(
        functools.partial(_mixout_kernel, dc=dc, dh=dh),
        out_shape=jax.ShapeDtypeStruct((m, d), F32),
        grid=(m // tm,),
        in_specs=[
            pl.BlockSpec((tm, d), lambda r: (r, 0)),
            pl.BlockSpec((tm, dc), lambda r: (r, 0)),
            pl.BlockSpec((tm, dh), lambda r: (r, 0)),
            pl.BlockSpec((tm, ds), lambda r: (r, 0)),
            pl.BlockSpec((None,) + nw.shape[1:], lambda r: (l, 0, 0)),
            pl.BlockSpec((None,) + w_out.shape[1:], lambda r: (l, 0, 0)),
        ],
        out_specs=pl.BlockSpec((tm, d), lambda r: (r, 0)),
        compiler_params=_cparams("parallel"),
        name="mixout",
    )(h, conv_o, hgrn_o, sb_o, nw, w_out)


def _ple_kernel(h_ref, p_ref, nw_ref, wu_ref, wg_ref, o_ref):
    h = h_ref[...]
    up = _dot(p_ref[...].astype(BF16), wu_ref[...])
    gate = _sigmoid(_dot(h.astype(BF16), wg_ref[...]))
    o_ref[...] = h + _rms(up * gate, nw_ref[6:7, :])


def _ple(h, p, nw, w_up, w_gate, l):
    m, d = h.shape
    dp = p.shape[-1]
    tm = min(_row_tile(m), 512)
    return pl.pallas_call(
        _ple_kernel,
        out_shape=jax.ShapeDtypeStruct((m, d), F32),
        grid=(m // tm,),
        in_specs=[
            pl.BlockSpec((tm, d), lambda r: (r, 0)),
            pl.BlockSpec((None, tm, dp), lambda r: (l, r, 0)),
            pl.BlockSpec((None,) + nw.shape[1:], lambda r: (l, 0, 0)),
            pl.BlockSpec((None,) + w_up.shape[1:], lambda r: (l, 0, 0)),
            pl.BlockSpec((None,) + w_gate.shape[1:], lambda r: (l, 0, 0)),
        ],
        out_specs=pl.BlockSpec((tm, d), lambda r: (r, 0)),
        compiler_params=_cparams("parallel"),
        name="ple",
    )(h, p, nw, w_up, w_gate)


def _layer(h, p_all, l, bsz, t, conv_prev, hgrn_prev, paged, prm):
    nw = prm["norm_w"]
    d_conv = prm["conv_w"].shape[-1]
    d_hgrn = prm["hgrn_norm_w"].shape[-1]
    d_sb = prm["sb_norm_w"].shape[-1]
    h = _ffn(h, nw, prm["ffn_w_gate"], prm["ffn_w_up"], prm["ffn_w_down"], l, 0, 0, 1)
    ab, hg, sq, sk, sv = _proj(h, nw, prm["w_in"], l, (2 * d_conv, 4 * d_hgrn, d_sb, d_sb, d_sb))

    conv_o, conv_state = _conv(ab.reshape(bsz, t, 2 * d_conv), conv_prev, prm["conv_w"], prm["conv_b"],
                               prm["conv_ln_w"], prm["conv_ln_b"], l)

    hg = hg.reshape(bsz, t, 4 * d_hgrn)
    chunk = HGRN_CHUNK if t >= HGRN_CHUNK else HGRN_MIN_CHUNK
    tp = -(-t // chunk) * chunk
    if tp != t:
        hg = jnp.pad(hg, ((0, 0), (0, tp - t), (0, 0)))
    hgrn_o, hgrn_state = _hgrn(hg, hgrn_prev, prm["hgrn_lb_logits"], prm["hgrn_norm_w"], l, t, chunk)
    hgrn_o = hgrn_o[:, :t]

    q3, k3, v3 = (a.reshape(bsz, t, d_sb) for a in (sq, sk, sv))
    if paged is None:
        sb_o = _sb_prompt(q3, k3, v3, prm["sb_bias"], prm["sb_norm_w"], l)
    else:
        cache_kt, cache_vt, page_table = paged
        tq = -(-t // 8) * 8
        qpad = jnp.pad(q3, ((0, 0), (0, tq - t), (0, 0)))
        kpad = jnp.pad(k3, ((0, 0), (0, SB_BLOCK - t), (0, 0)))
        vpad = jnp.pad(v3, ((0, 0), (0, SB_BLOCK - t), (0, 0)))
        sb_o = _sb_sample(qpad, kpad, vpad, cache_kt, cache_vt, page_table, prm["sb_bias"],
                          prm["sb_norm_w"], l)[:, :t]

    h = _mixout(h, conv_o.reshape(bsz * t, d_conv), hgrn_o.reshape(bsz * t, d_hgrn),
                sb_o.reshape(bsz * t, d_sb), nw, prm["w_out"], l)
    h = _ffn(h, nw, prm["ffn_w_gate"], prm["ffn_w_up"], prm["ffn_w_down"], l, 1, 4, 5)
    h = _ple(h, p_all, nw, prm["ple_w_up"], prm["ple_w_gate"], l)
    return h, (sk, sv, conv_state, hgrn_state)


def kernel(x_prompt, x_sample, cache_k, cache_v, state_conv, state_hgrn, page_table, p_prompt, p_sample, norm_w, ffn_w_gate, ffn_w_up, ffn_w_down, w_in, conv_w, conv_b, conv_ln_w, conv_ln_b, hgrn_lb_logits, hgrn_norm_w, sb_bias, sb_norm_w, w_out, ple_w_up, ple_w_gate):
    depth = w_in.shape[0]
    bp, tpr, dm = x_prompt.shape
    bs, ts, _ = x_sample.shape
    n_sb = sb_bias.shape[1]
    n_hg = state_hgrn.shape[2]
    d_conv = conv_w.shape[-1]
    prm = dict(norm_w=norm_w, ffn_w_gate=ffn_w_gate.astype(BF16), ffn_w_up=ffn_w_up.astype(BF16),
               ffn_w_down=ffn_w_down.astype(BF16), w_in=w_in.astype(BF16), conv_w=conv_w, conv_b=conv_b,
               conv_ln_w=conv_ln_w, conv_ln_b=conv_ln_b, hgrn_lb_logits=hgrn_lb_logits,
               hgrn_norm_w=hgrn_norm_w, sb_bias=sb_bias, sb_norm_w=sb_norm_w, w_out=w_out.astype(BF16),
               ple_w_up=ple_w_up.astype(BF16), ple_w_gate=ple_w_gate.astype(BF16))

    h_p = x_prompt.reshape(bp * tpr, dm)
    h_s = x_sample.reshape(bs * ts, dm)
    pp = p_prompt.reshape(depth, bp * tpr, -1)
    ps = p_sample.reshape(depth, bs * ts, -1)
    conv0 = jnp.zeros((bp, CONV_WIDTH - 1, d_conv), F32)
    hgrn0 = jnp.zeros((bp, n_hg // 2, LANES, LANES), F32)
    cache_kt = jnp.transpose(cache_k, (0, 1, 3, 4, 2))
    cache_vt = jnp.transpose(cache_v, (0, 1, 3, 4, 2))
    outs = [[] for _ in range(8)]
    for l in range(depth):
        h_p, st_p = _layer(h_p, pp, l, bp, tpr, conv0, hgrn0, None, prm)
        h_s, st_s = _layer(h_s, ps, l, bs, ts, state_conv[l], _pack_state(state_hgrn[l]),
                           (cache_kt, cache_vt, page_table), prm)
        for i, a in enumerate(st_p + st_s):
            outs[i].append(a)

    def heads(xs, b, t, n):
        return jnp.stack(xs).reshape(depth, b, t, n, HEAD_DIM)

    return (h_p.reshape(bp, tpr, dm), h_s.reshape(bs, ts, dm),
            heads(outs[0], bp, tpr, n_sb), heads(outs[1], bp, tpr, n_sb),
            jnp.stack(outs[2]), jnp.stack([_unpack_state(s) for s in outs[3]]),
            heads(outs[4], bs, ts, n_sb), heads(outs[5], bs, ts, n_sb),
            jnp.stack(outs[6]), jnp.stack([_unpack_state(s) for s in outs[7]]))
```

```python
import functools

import numpy as np
import jax
import jax.numpy as jnp
from jax import lax
from jax.experimental import pallas as pl
from jax.experimental.pallas import tpu as pltpu

F32 = jnp.float32
BF16 = jnp.bfloat16

EPS = 1e-6
HEAD_DIM = 64
LANES = 128
CONV_WIDTH = 31
CONV_TILE = 64
CONV_UNROLL = 4
CONV_PAD = 32
FFN_TF = 256
FFN_TF_WIDE_BELOW = 1024
HGRN_CHUNK = 64
HGRN_MIN_CHUNK = 16
SB_BLOCK = 128
SB_QROWS = 256
PAGES_PER_STEP = 16
PAGE_SLOTS = 3
VMEM_LIMIT = 48 * 1024 * 1024


def _cparams(*sem):
    return pltpu.CompilerParams(dimension_semantics=sem, vmem_limit_bytes=VMEM_LIMIT)


def _rms(x, w):
    ms = jnp.mean(x * x, axis=-1, keepdims=True)
    return x * lax.rsqrt(ms + EPS) * w


def _sigmoid(x):
    return 1.0 / (1.0 + jnp.exp(-x))


def _softplus(x):
    return jnp.maximum(x, 0.0) + jnp.log1p(jnp.exp(-jnp.abs(x)))


def _dot(a, b):
    return jnp.dot(a, b, preferred_element_type=F32)


def _dot_nt(a, b):
    return lax.dot_general(a, b, (((1,), (1,)), ((), ())), preferred_element_type=F32)


def _dot_tn(a, b):
    return lax.dot_general(a, b, (((0,), (0,)), ((), ())), preferred_element_type=F32)


def _split(x):
    hi = x.astype(BF16)
    lo = (x - hi.astype(F32)).astype(BF16)
    return hi, lo


def _dot_split(x, w):
    hi, lo = _split(x)
    return _dot(hi, w) + _dot(lo, w)


def _head_rms(o, g_ref, w):
    ms = _dot_split(o * o, g_ref[...])
    return o * lax.rsqrt(ms + EPS) * w


def _row_tile(m):
    for t in (1024, 512, 256, 128, 64, 32, 16, 8):
        if m % t == 0:
            return t
    raise ValueError(f"token count {m} must be a multiple of 8")


def _ffn_kernel(h_ref, nw_ref, wg_ref, wu_ref, wd_ref, o_ref, xn_ref, acc_ref, *, pre, post, nf):
    f = pl.program_id(1)

    @pl.when(f == 0)
    def _():
        xn_ref[...] = _rms(h_ref[...], nw_ref[pre:pre + 1, :]).astype(BF16)
        acc_ref[...] = jnp.zeros_like(acc_ref)

    xn = xn_ref[...]
    g = _dot(xn, wg_ref[...])
    u = _dot(xn, wu_ref[...])
    a = (g * _sigmoid(g) * u).astype(BF16)
    acc_ref[...] += _dot(a, wd_ref[...])

    @pl.when(f == nf - 1)
    def _():
        o_ref[...] = h_ref[...] + 0.5 * _rms(acc_ref[...], nw_ref[post:post + 1, :])


def _ffn(h, nw, wg, wu, wd, l, i, pre, post):
    m, d = h.shape
    dff = wg.shape[-1]
    tm = _row_tile(m)
    tf = FFN_TF if m >= FFN_TF_WIDE_BELOW else dff // 2
    nf = dff // tf
    return pl.pallas_call(
        functools.partial(_ffn_kernel, pre=pre, post=post, nf=nf),
        out_shape=jax.ShapeDtypeStruct((m, d), F32),
        grid=(m // tm, nf),
        in_specs=[
            pl.BlockSpec((tm, d), lambda r, f: (r, 0)),
            pl.BlockSpec((None,) + nw.shape[1:], lambda r, f: (l, 0, 0)),
            pl.BlockSpec((None, None, d, tf), lambda r, f: (l, i, 0, f)),
            pl.BlockSpec((None, None, d, tf), lambda r, f: (l, i, 0, f)),
            pl.BlockSpec((None, None, tf, d), lambda r, f: (l, i, f, 0)),
        ],
        out_specs=pl.BlockSpec((tm, d), lambda r, f: (r, 0)),
        scratch_shapes=[pltpu.VMEM((tm, d), BF16), pltpu.VMEM((tm, d), F32)],
        compiler_params=_cparams("parallel", "arbitrary"),
        name="ffn",
    )(h, nw, wg, wu, wd)


def _proj_kernel(h_ref, nw_ref, w_ref, conv_ref, hg_ref, q_ref, k_ref, v_ref, *, bounds):
    xn = _rms(h_ref[...], nw_ref[2:3, :]).astype(BF16)
    for ref, (lo, hi) in zip((conv_ref, hg_ref, q_ref, k_ref, v_ref), bounds):
        ref[...] = _dot(xn, w_ref[:, lo:hi])


def _proj(h, nw, w_in, l, widths):
    m, d = h.shape
    tm = min(_row_tile(m), 512)
    bounds, off = [], 0
    for w in widths:
        bounds.append((off, off + w))
        off += w
    return pl.pallas_call(
        functools.partial(_proj_kernel, bounds=tuple(bounds)),
        out_shape=[jax.ShapeDtypeStruct((m, w), F32) for w in widths],
        grid=(m // tm,),
        in_specs=[
            pl.BlockSpec((tm, d), lambda r: (r, 0)),
            pl.BlockSpec((None,) + nw.shape[1:], lambda r: (l, 0, 0)),
            pl.BlockSpec((None,) + w_in.shape[1:], lambda r: (l, 0, 0)),
        ],
        out_specs=[pl.BlockSpec((tm, w), lambda r: (r, 0)) for w in widths],
        compiler_params=_cparams("parallel"),
        name="proj",
    )(h, nw, w_in)


def _conv_kernel(ab_ref, prev_ref, w_ref, b_ref, lnw_ref, lnb_ref, y_ref, st_ref, up_ref, *, t, rt, dc):
    npad = CONV_WIDTH - 1
    first = CONV_PAD - npad
    a = ab_ref[:, 0:dc]
    b = ab_ref[:, dc:2 * dc]
    up_ref[0:first, :] = jnp.zeros((first, dc), F32)
    up_ref[first:CONV_PAD, :] = prev_ref[...]
    up_ref[CONV_PAD:CONV_PAD + t, :] = a * _sigmoid(b)
    st_ref[...] = up_ref[first + t:first + t + npad, :]

    groups = [[] for _ in range(8)]
    for j in range(CONV_WIDTH):
        blk8, rem = divmod(j + first, 8)
        groups[rem].append((blk8, j))

    def tile(base):
        y = None
        for rem, taps in enumerate(groups):
            if not taps:
                continue
            n = rt + 8 if rem else rt
            part = None
            for blk8, j in taps:
                term = w_ref[j:j + 1, :] * up_ref[pl.ds(base + 8 * blk8, n), :]
                part = term if part is None else part + term
            if rem:
                part = part[rem:rem + rt]
            y = part if y is None else y + part
        y = y + b_ref[...]
        mu = jnp.mean(y, axis=-1, keepdims=True)
        yc = y - mu
        var = jnp.mean(yc * yc, axis=-1, keepdims=True)
        yn = yc * lax.rsqrt(var + EPS) * lnw_ref[...] + lnb_ref[...]
        y_ref[pl.ds(base, rt), :] = yn * _sigmoid(yn)

    if t == rt:
        tile(0)
    else:
        def body(i, _):
            for u in range(CONV_UNROLL):
                tile(pl.multiple_of((i * CONV_UNROLL + u) * rt, rt))
            return 0
        lax.fori_loop(0, t // (rt * CONV_UNROLL), body, 0)


def _conv(ab, prev, w, b, lnw, lnb, l):
    bsz, t, dc2 = ab.shape
    dc = dc2 // 2
    rt = CONV_TILE if t % (CONV_TILE * CONV_UNROLL) == 0 else t
    vec = lambda a: pl.BlockSpec((None, 1, dc), lambda s: (l, 0, 0))
    return pl.pallas_call(
        functools.partial(_conv_kernel, t=t, rt=rt, dc=dc),
        out_shape=[jax.ShapeDtypeStruct((bsz, t, dc), F32),
                   jax.ShapeDtypeStruct((bsz, CONV_WIDTH - 1, dc), F32)],
        grid=(bsz,),
        in_specs=[
            pl.BlockSpec((None, t, dc2), lambda s: (s, 0, 0)),
            pl.BlockSpec((None, CONV_WIDTH - 1, dc), lambda s: (s, 0, 0)),
            pl.BlockSpec((None, CONV_WIDTH, dc), lambda s: (l, 0, 0)),
            vec(b), vec(lnw), vec(lnb),
        ],
        out_specs=[pl.BlockSpec((None, t, dc), lambda s: (s, 0, 0)),
                   pl.BlockSpec((None, CONV_WIDTH - 1, dc), lambda s: (s, 0, 0))],
        scratch_shapes=[pltpu.VMEM((CONV_PAD + t, dc), F32)],
        compiler_params=_cparams("parallel"),
        name="conv",
    )(ab, prev, w, b.reshape(b.shape[0], 1, dc), lnw.reshape(lnw.shape[0], 1, dc),
      lnb.reshape(lnb.shape[0], 1, dc))


def _hgrn_consts(c):
    levels = []
    m = c // 2
    while m >= 1:
        levels.append(m)
        m //= 2
    idx = np.arange(c)
    mats = [(idx[None, :] <= idx[:, None]).astype(np.float32)]
    masks = []
    for m in levels:
        blk, half = idx // (2 * m), (idx // m) % 2
        same = blk[:, None] == blk[None, :]
        upper = same & (half[:, None] == 1) & (half[None, :] == 1) & (idx[None, :] <= idx[:, None])
        lower = same & (half[:, None] == 0) & (half[None, :] == 0) & (idx[None, :] > idx[:, None])
        mats += [upper.astype(np.float32), lower.astype(np.float32)]
        masks.append((same & (half[:, None] == 1) & (half[None, :] == 0)).astype(np.float32))
    masks.append(np.eye(c, dtype=np.float32))
    masks = np.stack(masks)
    masks = np.concatenate([masks, masks], axis=1)
    return len(levels), np.concatenate(mats, axis=0), masks


def _head_avg(d):
    i = np.arange(d) // HEAD_DIM
    return (i[:, None] == i[None, :]).astype(np.float32) / HEAD_DIM


def _hgrn_kernel(hg_ref, s0_ref, lbl_ref, nw_ref, mall_ref, mask_ref, g_ref, o_ref, sT_ref, st_ref,
                 *, l, c, cps, d, nlev, nsteps, t_valid):
    step = pl.program_id(1)
    npair = d // LANES

    @pl.when(step == 0)
    def _():
        st_ref[...] = s0_ref[...]

    lg = lbl_ref[...]
    e = jnp.exp(lg - jnp.max(lg, axis=0, keepdims=True))
    sm = e / jnp.sum(e, axis=0, keepdims=True)
    row = lax.broadcasted_iota(jnp.int32, sm.shape, 0)
    lb = jnp.sum(jnp.where((row >= 1) & (row <= l), sm, 0.0), axis=0, keepdims=True)
    log_lb = jnp.log(lb)
    log_1mlb = jnp.log1p(-lb)

    ones16 = jnp.ones((2 * c, LANES), BF16)
    lane = lax.broadcasted_iota(jnp.int32, (c, LANES), 1)
    first = lane < HEAD_DIM
    ri = lax.broadcasted_iota(jnp.int32, (LANES, LANES), 0) // HEAD_DIM
    cj = lax.broadcasted_iota(jnp.int32, (LANES, LANES), 1) // HEAD_DIM
    blockdiag = ri == cj
    mall = mall_ref[...]

    prep = []
    for cc in range(cps):
        rows = slice(cc * c, (cc + 1) * c)
        q = hg_ref[rows, 0:d]
        zf = hg_ref[rows, d:2 * d]

        bb = log_1mlb - _softplus(-zf)
        log_f = jnp.maximum(log_lb, bb) + jnp.log1p(jnp.exp(-jnp.abs(log_lb - bb)))
        kk = (1.0 - lb) * _sigmoid(-zf)
        if t_valid < nsteps * cps * c:
            trow = lax.broadcasted_iota(jnp.int32, (c, d), 0) + (step * cps + cc) * c
            live = trow < t_valid
            log_f = jnp.where(live, log_f, 0.0)
            kk = jnp.where(live, kk, 0.0)
        lf2 = jnp.concatenate(_split(log_f), axis=0)
        prep.append((q, kk, lf2))

    e_alls = [_dot(mall, lf2) for _, _, lf2 in prep]

    amats = {}
    for i in range(nlev + 1):
        for cc in range(cps):
            q, kk, _ = prep[cc]
            e_all = e_alls[cc]
            for p in range(npair):
                sl = slice(p * LANES, (p + 1) * LANES)
                if i < nlev:
                    ql = q[:, sl] * jnp.exp(e_all[(1 + 2 * i) * c:(2 + 2 * i) * c, sl])
                    kl = kk[:, sl] * jnp.exp(e_all[(2 + 2 * i) * c:(3 + 2 * i) * c, sl])
                else:
                    ql, kl = q[:, sl], kk[:, sl]
                qst = jnp.concatenate([jnp.where(first, ql, 0.0), jnp.where(first, 0.0, ql)], axis=0)
                term = mask_ref[i] * _dot_nt(qst.astype(BF16), kl.astype(BF16))
                amats[(cc, p)] = term if i == 0 else amats[(cc, p)] + term

    for cc in range(cps):
        rows = slice(cc * c, (cc + 1) * c)
        q, kk, lf2 = prep[cc]
        bcum = e_alls[cc][0:c]
        b_last = bcum[c - 1:c, :]
        qe = (q * jnp.exp(bcum)).astype(BF16)
        kd = (kk * jnp.exp(b_last - bcum)).astype(BF16)
        v16 = hg_ref[rows, 2 * d:3 * d].astype(BF16)
        gate = hg_ref[rows, 3 * d:4 * d]
        outs = []
        for p in range(npair):
            sl = slice(p * LANES, (p + 1) * LANES)
            a16 = amats[(cc, p)].astype(BF16)
            vp = v16[:, sl]
            zero = jnp.zeros_like(vp)
            state = st_ref[p]
            o = _dot(qe[:, sl], state.astype(BF16))
            o = o + _dot(a16[0:c], jnp.where(first, vp, zero)) + _dot(a16[c:2 * c], jnp.where(first, zero, vp))
            outs.append(o)
            elog = _dot_tn(lf2[:, sl], ones16)
            upd = _dot_tn(kd[:, sl], vp)
            st_ref[p] = jnp.exp(elog) * state + jnp.where(blockdiag, upd, 0.0)
        o = jnp.concatenate(outs, axis=1)
        o_ref[rows, :] = _head_rms(o, g_ref, nw_ref[...]) * (gate * _sigmoid(gate))

    @pl.when(step == nsteps - 1)
    def _():
        sT_ref[...] = st_ref[...]


def _hgrn(hg, s0p, lb_logits, norm_w, l, t_valid, c):
    bsz, t, d4 = hg.shape
    d = d4 // 4
    cps = 2 if t % (2 * c) == 0 else 1
    nsteps = t // (c * cps)
    nlev, mall, masks = _hgrn_consts(c)
    npair = d // LANES
    full = lambda a: pl.BlockSpec(a.shape, lambda s, ci: (0,) * a.ndim)
    mall = jnp.asarray(np.concatenate([mall, mall], axis=1), BF16)
    masks = jnp.asarray(masks, F32)
    gavg = jnp.asarray(_head_avg(d), BF16)
    return pl.pallas_call(
        functools.partial(_hgrn_kernel, l=l, c=c, cps=cps, d=d, nlev=nlev, nsteps=nsteps, t_valid=t_valid),
        out_shape=[jax.ShapeDtypeStruct((bsz, t, d), F32),
                   jax.ShapeDtypeStruct((bsz, npair, LANES, LANES), F32)],
        grid=(bsz, nsteps),
        in_specs=[
            pl.BlockSpec((None, c * cps, d4), lambda s, ci: (s, ci, 0)),
            pl.BlockSpec((None, npair, LANES, LANES), lambda s, ci: (s, 0, 0, 0)),
            full(lb_logits),
            pl.BlockSpec((None, 1, d), lambda s, ci: (l, 0, 0)),
            full(mall), full(masks), full(gavg),
        ],
        out_specs=[pl.BlockSpec((None, c * cps, d), lambda s, ci: (s, ci, 0)),
                   pl.BlockSpec((None, npair, LANES, LANES), lambda s, ci: (s, 0, 0, 0))],
        scratch_shapes=[pltpu.VMEM((npair, LANES, LANES), F32)],
        compiler_params=_cparams("parallel", "arbitrary"),
        name="hgrn",
    )(hg, s0p, lb_logits, norm_w.reshape(norm_w.shape[0], 1, d), mall, masks, gavg)


def _pack_state(s):
    bsz, h = s.shape[:2]
    s = s.reshape(bsz, h // 2, 2, HEAD_DIM, HEAD_DIM)
    z = jnp.zeros_like(s[:, :, 0])
    top = jnp.concatenate([s[:, :, 0], z], axis=-1)
    bot = jnp.concatenate([z, s[:, :, 1]], axis=-1)
    return jnp.concatenate([top, bot], axis=-2)


def _unpack_state(sp):
    a = sp[:, :, :HEAD_DIM, :HEAD_DIM]
    b = sp[:, :, HEAD_DIM:, HEAD_DIM:]
    bsz, npair = sp.shape[:2]
    return jnp.stack([a, b], axis=2).reshape(bsz, 2 * npair, HEAD_DIM, HEAD_DIM)


LOG2E = 1.4426950408889634


def _sb_consts():
    j = np.arange(SB_BLOCK)
    after = (j[:, None] > j[None, :]).astype(np.float32)
    uo = np.concatenate([after, np.ones_like(after)], axis=1)
    return np.concatenate([uo, uo], axis=0)


def _softplus2(z):
    return jnp.maximum(z, 0.0) + jnp.log2(1.0 + jnp.exp2(-jnp.abs(z)))


def _sb_stage(chains, masks, carries, uo_ref):
    r = chains[0][0].shape[0]
    t1s, lhs = [], []
    for zs, ms in zip(chains, masks):
        for z, m in zip(zs, ms):
            sp = _softplus2(z)
            t1s.append(z - sp)
            if m is not None:
                sp = jnp.where(m, sp, 0.0)
            hi, lo = _split(sp)
            lhs.append(jnp.concatenate([hi, lo], axis=1))
    both = _dot(lhs[0] if len(lhs) == 1 else jnp.concatenate(lhs, axis=0), uo_ref[...])
    out, new_carries, n = [], [], 0
    for zs, ms, c in zip(chains, masks, carries):
        ws = []
        for m in ms:
            b = both[n * r:(n + 1) * r]
            a = jnp.exp2(t1s[n] - (c + b[:, 0:SB_BLOCK]))
            if m is not None:
                a = jnp.where(m, a, 0.0)
            ws.append(a.astype(BF16))
            c = c + b[:, SB_BLOCK:2 * SB_BLOCK]
            n += 1
        out.append(ws)
        new_carries.append(c)
    return out, new_carries


def _sbp_kernel(bias_ref, q_ref, k_ref, v_ref, nw_ref, uo_ref, g_ref, o_ref,
                k16_ref, v2_ref, qm_ref, acc_ref, car_ref, *, l, d):
    qi = pl.program_id(1)
    blk = SB_BLOCK
    npair = d // LANES
    nhalf = SB_QROWS // blk
    first = lax.broadcasted_iota(jnp.int32, (1, LANES), 1) < HEAD_DIM

    @pl.when(qi == 0)
    def _():
        k16_ref[...] = k_ref[...].astype(BF16)
        nkb = v2_ref.shape[0]
        for p in range(npair):
            vv = v_ref[:, p * LANES:(p + 1) * LANES].reshape(nkb, blk, LANES)
            v2_ref[:, p, 0:blk, :] = jnp.where(first, vv, 0.0).astype(BF16)
            v2_ref[:, p, blk:2 * blk, :] = jnp.where(first, 0.0, vv).astype(BF16)

    for p in range(npair):
        for hf in range(nhalf):
            qs = q_ref[hf * blk:(hf + 1) * blk, p * LANES:(p + 1) * LANES] * (HEAD_DIM ** -0.5 * LOG2E)
            qm_ref[p, hf, 0:blk, :] = jnp.where(first, qs, 0.0).astype(BF16)
            qm_ref[p, hf, blk:2 * blk, :] = jnp.where(first, 0.0, qs).astype(BF16)

    def chain_id(p, hf, par):
        return (p * nhalf + hf) * 2 + par

    def stage(plan, carries):
        chains, masks, keys = [], [], []
        for p in range(npair):
            sl = slice(p * LANES, (p + 1) * LANES)
            for hf in range(nhalf):
                tiles = plan[(p, hf)]
                zs = [_dot_nt(qm_ref[p, hf], k16_ref[pl.ds(pl.multiple_of(kb * blk, blk), blk), sl]) for kb, _ in tiles]
                for par in range(2):
                    bias = bias_ref[l, 2 * p + par] * LOG2E
                    chains.append([z[par * blk:(par + 1) * blk] + bias for z in zs])
                    masks.append([m for _, m in tiles])
                    keys.append(chain_id(p, hf, par))
        ws, cs = _sb_stage(chains, masks, [carries[c] for c in keys], uo_ref)
        outs, n = {}, 0
        for p in range(npair):
            for hf in range(nhalf):
                o = None
                for i, (kb, _) in enumerate(plan[(p, hf)]):
                    t = _dot(jnp.concatenate([ws[n][i], ws[n + 1][i]], axis=1), v2_ref[kb, p])
                    o = t if o is None else o + t
                outs[(p, hf)] = o
                n += 2
        return outs, dict(zip(keys, cs))

    ti = lax.broadcasted_iota(jnp.int32, (blk, blk), 0)
    si = lax.broadcasted_iota(jnp.int32, (blk, blk), 1)
    causal = si < ti
    zero = jnp.zeros((blk, LANES), F32)
    nchain = npair * nhalf * 2

    d0 = qi * nhalf
    d1 = d0 + 1
    plan = {}
    for p in range(npair):
        plan[(p, 0)] = [(d0, causal)]
        plan[(p, 1)] = [(d1, causal), (d0, None)]
    outs, cs = stage(plan, {c: zero for c in range(nchain)})
    for (p, hf), o in outs.items():
        acc_ref[hf * blk:(hf + 1) * blk, p * LANES:(p + 1) * LANES] = o
    for c, v in cs.items():
        car_ref[c] = v

    def body(j, _):
        kb = d0 - 1 - j
        plan = {(p, hf): [(kb, None)] for p in range(npair) for hf in range(nhalf)}
        outs, cs = stage(plan, {c: car_ref[c] for c in range(nchain)})
        for (p, hf), o in outs.items():
            acc_ref[hf * blk:(hf + 1) * blk, p * LANES:(p + 1) * LANES] += o
        for c, v in cs.items():
            car_ref[c] = v
        return 0

    lax.fori_loop(0, nhalf * qi, body, 0)
    o_ref[...] = _head_rms(acc_ref[...], g_ref, nw_ref[...])


def _sb_prompt(q, k, v, bias, norm_w, l):
    bsz, t, d = q.shape
    blk = SB_BLOCK
    qr = SB_QROWS
    npair = d // LANES
    uo = jnp.asarray(_sb_consts(), BF16)
    gavg = jnp.asarray(_head_avg(d), BF16)
    full = lambda a: pl.BlockSpec(a.shape, lambda s, qi: (0,) * a.ndim)
    return pl.pallas_call(
        functools.partial(_sbp_kernel, l=l, d=d),
        out_shape=jax.ShapeDtypeStruct((bsz, t, d), F32),
        grid=(bsz, t // qr),
        in_specs=[
            pl.BlockSpec(memory_space=pltpu.SMEM),
            pl.BlockSpec((None, qr, d), lambda s, qi: (s, qi, 0)),
            pl.BlockSpec((None, t, d), lambda s, qi: (s, 0, 0)),
            pl.BlockSpec((None, t, d), lambda s, qi: (s, 0, 0)),
            pl.BlockSpec((None, 1, d), lambda s, qi: (l, 0, 0)),
            full(uo), full(gavg),
        ],
        out_specs=pl.BlockSpec((None, qr, d), lambda s, qi: (s, qi, 0)),
        scratch_shapes=[pltpu.VMEM((t, d), BF16), pltpu.VMEM((t // blk, npair, 2 * blk, LANES), BF16),
                        pltpu.VMEM((npair, qr // blk, 2 * blk, LANES), BF16), pltpu.VMEM((qr, d), F32),
                        pltpu.VMEM((npair * (qr // blk) * 2, blk, LANES), F32)],
        compiler_params=_cparams("parallel", "arbitrary"),
        name="sb_prompt",
    )(bias, q, k, v, norm_w.reshape(norm_w.shape[0], 1, d), uo, gavg)


def _sbs_kernel(pt_ref, bias_ref, q_ref, kn_ref, vn_ref, nw_ref, uo_ref, g_ref, ck_hbm, cv_hbm, o_ref,
                kbuf, vbuf, sem, qbd_ref, acc_ref, car_ref, *, l, d, nh, tq, nstep, nseq, npages):
    npg = PAGES_PER_STEP
    s = pl.program_id(0)
    j = pl.program_id(1)
    step = s * nstep + j
    slot = step % PAGE_SLOTS
    rows = nh * tq
    blk = SB_BLOCK

    def page_copies(linear):
        seq, jj, slot_ = linear // nstep, linear % nstep, linear % PAGE_SLOTS
        out = []
        for i in range(npg):
            page = pt_ref[seq, npages - 1 - (jj * npg + i)]
            out.append((pltpu.make_async_copy(ck_hbm.at[l, page], kbuf.at[slot_, i], sem.at[0, slot_]),
                        pltpu.make_async_copy(cv_hbm.at[l, page], vbuf.at[slot_, i], sem.at[1, slot_])))
        return out

    def start(linear):
        for ck, cv in page_copies(linear):
            ck.start(priority=0)
            cv.start(priority=1)

    @pl.when(step == 0)
    def _():
        for ahead in range(PAGE_SLOTS - 1):
            start(ahead)

    @pl.when(step + (PAGE_SLOTS - 1) < nseq * nstep)
    def _():
        start(step + (PAGE_SLOTS - 1))

    for ck, cv in page_copies(step):
        ck.wait()
        cv.wait()

    lane_d = lax.broadcasted_iota(jnp.int32, (rows, d), 1) // HEAD_DIM
    rgrp = lax.broadcasted_iota(jnp.int32, (rows, d), 0) // tq
    own = lane_d == rgrp

    rg1 = lax.broadcasted_iota(jnp.int32, (rows, blk), 0) // tq
    bias = jnp.zeros((rows, blk), F32)
    for h in range(nh):
        bias = jnp.where(rg1 == h, bias_ref[l, h] * LOG2E, bias)

    @pl.when(j == 0)
    def _():
        qs = q_ref[...] * (HEAD_DIM ** -0.5 * LOG2E)
        qbd = jnp.where(own, jnp.concatenate([qs] * nh, axis=0), 0.0).astype(BF16)
        qbd_ref[...] = qbd
        tpos = lax.broadcasted_iota(jnp.int32, (rows, blk), 0) % tq
        spos = lax.broadcasted_iota(jnp.int32, (rows, blk), 1)
        z = _dot_nt(qbd, kn_ref[...].astype(BF16)) + bias
        ws, cs = _sb_stage([[z]], [[spos < tpos]], [jnp.zeros((rows, blk), F32)], uo_ref)
        acc_ref[...] = _dot(ws[0][0], vn_ref[...].astype(BF16))
        car_ref[...] = cs[0]

    qbd = qbd_ref[...]
    zs = [_dot(qbd, kbuf[slot, i].reshape(d, blk).astype(BF16)) + bias for i in range(npg)]
    ws, cs = _sb_stage([zs], [[None] * npg], [car_ref[...]], uo_ref)
    acc = acc_ref[...]
    for i in range(npg):
        acc = acc + _dot_nt(ws[0][i], vbuf[slot, i].reshape(d, blk).astype(BF16))
    acc_ref[...] = acc
    car_ref[...] = cs[0]

    @pl.when(j == nstep - 1)
    def _():
        full = jnp.where(own, acc_ref[...], 0.0)
        o = full[0:tq]
        for h in range(1, nh):
            o = o + full[h * tq:(h + 1) * tq]
        o_ref[...] = _head_rms(o, g_ref, nw_ref[...])


def _sb_sample(q, k_new, v_new, cache_kt, cache_vt, page_table, bias, norm_w, l):
    bsz, tq, d = q.shape
    nh = d // HEAD_DIM
    npages = page_table.shape[1]
    npg = PAGES_PER_STEP
    nstep = npages // npg
    blk = SB_BLOCK
    uo = jnp.asarray(_sb_consts(), BF16)
    gavg = jnp.asarray(_head_avg(d), BF16)
    full = lambda a: pl.BlockSpec(a.shape, lambda s, j, pt: (0,) * a.ndim)
    grid_spec = pltpu.PrefetchScalarGridSpec(
        num_scalar_prefetch=1,
        grid=(bsz, nstep),
        in_specs=[
            pl.BlockSpec(memory_space=pltpu.SMEM),
            pl.BlockSpec((None, tq, d), lambda s, j, pt: (s, 0, 0)),
            pl.BlockSpec((None, blk, d), lambda s, j, pt: (s, 0, 0)),
            pl.BlockSpec((None, blk, d), lambda s, j, pt: (s, 0, 0)),
            pl.BlockSpec((None, 1, d), lambda s, j, pt: (l, 0, 0)),
            full(uo), full(gavg),
            pl.BlockSpec(memory_space=pl.ANY),
            pl.BlockSpec(memory_space=pl.ANY),
        ],
        out_specs=pl.BlockSpec((None, tq, d), lambda s, j, pt: (s, 0, 0)),
        scratch_shapes=[pltpu.VMEM((PAGE_SLOTS, npg, nh, HEAD_DIM, blk), F32),
                        pltpu.VMEM((PAGE_SLOTS, npg, nh, HEAD_DIM, blk), F32),
                        pltpu.SemaphoreType.DMA((2, PAGE_SLOTS)),
                        pltpu.VMEM((nh * tq, d), BF16), pltpu.VMEM((nh * tq, d), F32),
                        pltpu.VMEM((nh * tq, blk), F32)],
    )
    return pl.pallas_call(
        functools.partial(_sbs_kernel, l=l, d=d, nh=nh, tq=tq, nstep=nstep, nseq=bsz, npages=npages),
        out_shape=jax.ShapeDtypeStruct((bsz, tq, d), F32),
        grid_spec=grid_spec,
        compiler_params=_cparams("arbitrary", "arbitrary"),
        name="sb_sample",
    )(page_table, bias, q, k_new, v_new, norm_w.reshape(norm_w.shape[0], 1, d), uo, gavg, cache_kt, cache_vt)


def _mixout_kernel(h_ref, c_ref, g_ref, s_ref, nw_ref, w_ref, o_ref, *, dc, dh):
    mix = _dot(c_ref[...].astype(BF16), w_ref[0:dc, :])
    mix = mix + _dot(g_ref[...].astype(BF16), w_ref[dc:dc + dh, :])
    mix = mix + _dot(s_ref[...].astype(BF16), w_ref[dc + dh:, :])
    o_ref[...] = h_ref[...] + _rms(mix, nw_ref[3:4, :])


def _mixout(h, conv_o, hgrn_o, sb_o, nw, w_out, l):
    m, d = h.shape
    tm = min(_row_tile(m), 512)
    dc, dh, ds = conv_o.shape[1], hgrn_o.shape[1], sb_o.shape[1]
    return pl.pallas_call(
        functools.partial(_mixout_kernel, dc=dc, dh=dh),
        out_shape=jax.ShapeDtypeStruct((m, d), F32),
        grid=(m // tm,),
        in_specs=[
            pl.BlockSpec((tm, d), lambda r: (r, 0)),
            pl.BlockSpec((tm, dc), lambda r: (r, 0)),
            pl.BlockSpec((tm, dh), lambda r: (r, 0)),
            pl.BlockSpec((tm, ds), lambda r: (r, 0)),
            pl.BlockSpec((None,) + nw.shape[1:], lambda r: (l, 0, 0)),
            pl.BlockSpec((None,) + w_out.shape[1:], lambda r: (l, 0, 0)),
        ],
        out_specs=pl.BlockSpec((tm, d), lambda r: (r, 0)),
        compiler_params=_cparams("parallel"),
        name="mixout",
    )(h, conv_o, hgrn_o, sb_o, nw, w_out)


def _ple_kernel(h_ref, p_ref, nw_ref, wu_ref, wg_ref, o_ref):
    h = h_ref[...]
    up = _dot(p_ref[...].astype(BF16), wu_ref[...])
    gate = _sigmoid(_dot(h.astype(BF16), wg_ref[...]))
    o_ref[...] = h + _rms(up * gate, nw_ref[6:7, :])


def _ple(h, p, nw, w_up, w_gate, l):
    m, d = h.shape
    dp = p.shape[-1]
    tm = min(_row_tile(m), 512)
    return pl.pallas_call(
        _ple_kernel,
        out_shape=jax.ShapeDtypeStruct((m, d), F32),
        grid=(m // tm,),
        in_specs=[
            pl.BlockSpec((tm, d), lambda r: (r, 0)),
            pl.BlockSpec((None, tm, dp), lambda r: (l, r, 0)),
            pl.BlockSpec((None,) + nw.shape[1:], lambda r: (l, 0, 0)),
            pl.BlockSpec((None,) + w_up.shape[1:], lambda r: (l, 0, 0)),
            pl.BlockSpec((None,) + w_gate.shape[1:], lambda r: (l, 0, 0)),
        ],
        out_specs=pl.BlockSpec((tm, d), lambda r: (r, 0)),
        compiler_params=_cparams("parallel"),
        name="ple",
    )(h, p, nw, w_up, w_gate)


def _layer(h, p_all, l, bsz, t, conv_prev, hgrn_prev, paged, prm):
    nw = prm["norm_w"]
    d_conv = prm["conv_w"].shape[-1]
    d_hgrn = prm["hgrn_norm_w"].shape[-1]
    d_sb = prm["sb_norm_w"].shape[-1]
    h = _ffn(h, nw, prm["ffn_w_gate"], prm["ffn_w_up"], prm["ffn_w_down"], l, 0, 0, 1)
    ab, hg, sq, sk, sv = _proj(h, nw, prm["w_in"], l, (2 * d_conv, 4 * d_hgrn, d_sb, d_sb, d_sb))

    conv_o, conv_state = _conv(ab.reshape(bsz, t, 2 * d_conv), conv_prev, prm["conv_w"], prm["conv_b"],
                               prm["conv_ln_w"], prm["conv_ln_b"], l)

    hg = hg.reshape(bsz, t, 4 * d_hgrn)
    chunk = HGRN_CHUNK if t >= HGRN_CHUNK else HGRN_MIN_CHUNK
    tp = -(-t // chunk) * chunk
    if tp != t:
        hg = jnp.pad(hg, ((0, 0), (0, tp - t), (0, 0)))
    hgrn_o, hgrn_state = _hgrn(hg, hgrn_prev, prm["hgrn_lb_logits"], prm["hgrn_norm_w"], l, t, chunk)
    hgrn_o = hgrn_o[:, :t]

    q3, k3, v3 = (a.reshape(bsz, t, d_sb) for a in (sq, sk, sv))
    if paged is None:
        sb_o = _sb_prompt(q3, k3, v3, prm["sb_bias"], prm["sb_norm_w"], l)
    else:
        cache_kt, cache_vt, page_table = paged
        tq = -(-t // 8) * 8
        qpad = jnp.pad(q3, ((0, 0), (0, tq - t), (0, 0)))
        kpad = jnp.pad(k3, ((0, 0), (0, SB_BLOCK - t), (0, 0)))
        vpad = jnp.pad(v3, ((0, 0), (0, SB_BLOCK - t), (0, 0)))
        sb_o = _sb_sample(qpad, kpad, vpad, cache_kt, cache_vt, page_table, prm["sb_bias"],
                          prm["sb_norm_w"], l)[:, :t]

    h = _mixout(h, conv_o.reshape(bsz * t, d_conv), hgrn_o.reshape(bsz * t, d_hgrn),
                sb_o.reshape(bsz * t, d_sb), nw, prm["w_out"], l)
    h = _ffn(h, nw, prm["ffn_w_gate"], prm["ffn_w_up"], prm["ffn_w_down"], l, 1, 4, 5)
    h = _ple(h, p_all, nw, prm["ple_w_up"], prm["ple_w_gate"], l)
    return h, (sk, sv, conv_state, hgrn_state)


def kernel(x_prompt, x_sample, cache_k, cache_v, state_conv, state_hgrn, page_table, p_prompt, p_sample, norm_w, ffn_w_gate, ffn_w_up, ffn_w_down, w_in, conv_w, conv_b, conv_ln_w, conv_ln_b, hgrn_lb_logits, hgrn_norm_w, sb_bias, sb_norm_w, w_out, ple_w_up, ple_w_gate):
    depth = w_in.shape[0]
    bp, tpr, dm = x_prompt.shape
    bs, ts, _ = x_sample.shape
    n_sb = sb_bias.shape[1]
    n_hg = state_hgrn.shape[2]
    d_conv = conv_w.shape[-1]
    prm = dict(norm_w=norm_w, ffn_w_gate=ffn_w_gate.astype(BF16), ffn_w_up=ffn_w_up.astype(BF16),
               ffn_w_down=ffn_w_down.astype(BF16), w_in=w_in.astype(BF16), conv_w=conv_w, conv_b=conv_b,
               conv_ln_w=conv_ln_w, conv_ln_b=conv_ln_b, hgrn_lb_logits=hgrn_lb_logits,
               hgrn_norm_w=hgrn_norm_w, sb_bias=sb_bias, sb_norm_w=sb_norm_w, w_out=w_out.astype(BF16),
               ple_w_up=ple_w_up.astype(BF16), ple_w_gate=ple_w_gate.astype(BF16))

    h_p = x_prompt.reshape(bp * tpr, dm)
    h_s = x_sample.reshape(bs * ts, dm)
    pp = p_prompt.reshape(depth, bp * tpr, -1)
    ps = p_sample.reshape(depth, bs * ts, -1)
    conv0 = jnp.zeros((bp, CONV_WIDTH - 1, d_conv), F32)
    hgrn0 = jnp.zeros((bp, n_hg // 2, LANES, LANES), F32)
    cache_kt = jnp.transpose(cache_k, (0, 1, 3, 4, 2))
    cache_vt = jnp.transpose(cache_v, (0, 1, 3, 4, 2))
    outs = [[] for _ in range(8)]
    for l in range(depth):
        h_p, st_p = _layer(h_p, pp, l, bp, tpr, conv0, hgrn0, None, prm)
        h_s, st_s = _layer(h_s, ps, l, bs, ts, state_conv[l], _pack_state(state_hgrn[l]),
                           (cache_kt, cache_vt, page_table), prm)
        for i, a in enumerate(st_p + st_s):
            outs[i].append(a)

    def heads(xs, b, t, n):
        return jnp.stack(xs).reshape(depth, b, t, n, HEAD_DIM)

    return (h_p.reshape(bp, tpr, dm), h_s.reshape(bs, ts, dm),
            heads(outs[0], bp, tpr, n_sb), heads(outs[1], bp, tpr, n_sb),
            jnp.stack(outs[2]), jnp.stack([_unpack_state(s) for s in outs[3]]),
            heads(outs[4], bs, ts, n_sb), heads(outs[5], bs, ts, n_sb),
            jnp.stack(outs[6]), jnp.stack([_unpack_state(s) for s in outs[7]]))
```
